```python
import math
import jax, jax.numpy as jnp
from jax import lax
import numpy as np

D_MODEL = 1024
BATCH = 2
SEQ = 8192
DEPTH = 2
DEC_BATCH = 128
DEC_SEQ = 4
PAST_LEN = 2048
PAGE_SIZE = 128

A_HEADS = 8
A_HEAD_DIM = 64
A_WIDTH = A_HEADS * A_HEAD_DIM
MOBA_BLOCK = 256
MOBA_TOPK = 3
Q_BLOCK = 128
B_HEADS = 8
B_DK = 64
B_DV = 64
B_QK = B_HEADS * B_DK
B_WIDTH = B_HEADS * B_DV
B_CONV_CH = 2 * B_QK + B_WIDTH
CONV_W = 4
GDN_CHUNK = 64
C_GROUPS = 4
C_CHUNK = 128
C_WIDTH = 512
C_GROUP_CH = C_WIDTH // C_GROUPS
N_BRANCH = 3
MEM_LEN = 256
X_HEADS = 4
X_HEAD_DIM = D_MODEL // X_HEADS
N_EXPERTS = 32
TOP_K = 4
D_FF = D_MODEL
SWIGLU_LIMIT = 7.0
SWIGLU_ALPHA = 1.702
MOE_BLOCK = 128
DN_ALPHA = (2 * DEPTH) ** 0.25
DN_BETA = (8 * DEPTH) ** -0.25
LN_EPS = 1e-5
RMS_EPS = 1e-6

OFF_AQ = 0
OFF_AK = OFF_AQ + A_WIDTH
OFF_AV = OFF_AK + A_WIDTH
OFF_BQKV = OFF_AV + A_WIDTH
OFF_BZ = OFF_BQKV + B_CONV_CH
OFF_BB = OFF_BZ + B_WIDTH
OFF_BA = OFF_BB + B_HEADS
OFF_CU = OFF_BA + B_HEADS
OFF_CV = OFF_CU + C_WIDTH
OFF_G = OFF_CV + C_WIDTH
P_IN = OFF_G + N_BRANCH * D_MODEL

kernel_name = 'hybrid_moba_gdn_sgu_moe_decoder_step'


def layer_norm(x, g, b):
    xf = x.astype(jnp.float32)
    mu = xf.mean(-1, keepdims=True)
    var = jnp.square(xf - mu).mean(-1, keepdims=True)
    return ((xf - mu) * lax.rsqrt(var + LN_EPS) * g + b).astype(x.dtype)


def l2norm(x):
    return x * lax.rsqrt(jnp.sum(x * x, -1, keepdims=True) + RMS_EPS)


def causal_conv(x, prev, w):
    xe = jnp.concatenate([prev, x], axis=1)
    t = x.shape[1]
    out = xe[:, 0:t] * w[0]
    for j in range(1, CONV_W):
        out = out + xe[:, j:j + t] * w[j]
    return jax.nn.silu(out), xe[:, -(CONV_W - 1):]


def moba_attention(q, k, v, q0):
    b, tq, h, dh = q.shape
    tk = k.shape[1]
    nb = -(-tk // MOBA_BLOCK)
    pad = nb * MOBA_BLOCK - tk

    def blocks(a):
        a = jnp.pad(a, ((0, 0), (0, pad), (0, 0), (0, 0)))
        return a.reshape(b, nb, MOBA_BLOCK, h, dh).transpose(0, 3, 1, 2, 4)

    kb, vb = blocks(k), blocks(v)
    k_mean = kb.astype(jnp.float32).mean(axis=3)
    n_sel = min(MOBA_TOPK, nb - 1)
    qbs = min(Q_BLOCK, tq)
    nq = tq // qbs
    qr = q.reshape(b, nq, qbs, h, dh).transpose(1, 0, 3, 2, 4)
    scale = dh ** -0.5
    b_idx = jnp.arange(b)[:, None, None, None]
    h_idx = jnp.arange(h)[None, :, None, None]
    blk_pos = jnp.arange(MOBA_BLOCK)
    nsb = n_sel * MOBA_BLOCK

    def attend(args):
        qc, c = args
        start = q0 + c * qbs
        own = start // MOBA_BLOCK
        pos = start + jnp.arange(qbs)
        k_own = lax.dynamic_index_in_dim(kb, own, axis=2, keepdims=False)
        v_own = lax.dynamic_index_in_dim(vb, own, axis=2, keepdims=False)
        s_own = jnp.einsum('bhqd,bhkd->bhqk', qc, k_own).astype(jnp.float32) * scale
        s_own = jnp.where(own * MOBA_BLOCK + blk_pos[None, :] <= pos[:, None], s_own, -jnp.inf)
        if n_sel == 0:
            p = jax.nn.softmax(s_own, -1).astype(q.dtype)
            return jnp.einsum('bhqk,bhkd->bhqd', p, v_own)
        gate = jnp.einsum('bhqd,bhnd->bhqn', qc.astype(jnp.float32), k_mean)
        gate = jnp.where(jnp.arange(nb) < own, gate, -jnp.inf)
        top_s, top_i = lax.top_k(gate, n_sel)
        valid = jnp.isfinite(top_s)
        k_sel = kb[b_idx, h_idx, top_i]
        v_sel = vb[b_idx, h_idx, top_i]
        s_sel = jnp.einsum('bhqd,bhqnkd->bhqnk', qc, k_sel).astype(jnp.float32) * scale
        s_sel = jnp.where(valid[..., None], s_sel, -jnp.inf).reshape(b, h, qbs, nsb)
        p = jax.nn.softmax(jnp.concatenate([s_sel, s_own], -1), -1).astype(q.dtype)
        p_sel = p[..., :nsb].reshape(b, h, qbs, n_sel, MOBA_BLOCK)
        return (jnp.einsum('bhqnk,bhqnkd->bhqd', p_sel, v_sel)
                + jnp.einsum('bhqk,bhkd->bhqd', p[..., nsb:], v_own))

    o = lax.map(attend, (qr, jnp.arange(nq)))
    return o.transpose(1, 0, 3, 2, 4).reshape(b, tq, h, dh)


def gated_delta_rule(q, k, v, g, beta, s0, chunk):
    b, t, h, dk = q.shape
    dv = v.shape[-1]
    n = t // chunk

    def split(a):
        a = a.reshape((b, n, chunk, h) + a.shape[3:])
        return jnp.moveaxis(a, (1, 3), (0, 2))

    qc, kc, vc, bc = split(q), split(k), split(v), split(beta)
    gc = jnp.cumsum(split(g), axis=-1)
    idx = jnp.arange(chunk)
    tril = idx[:, None] >= idx[None, :]
    strict = idx[:, None] > idx[None, :]
    decay = jnp.exp(jnp.where(tril, gc[..., :, None] - gc[..., None, :], -jnp.inf))
    kbeta = kc * bc[..., None]
    a_mat = jnp.where(strict, jnp.einsum('nbhid,nbhjd->nbhij', kbeta, kc) * decay, 0.0)
    eye = jnp.eye(chunk, dtype=jnp.float32)
    t_mat = lax.linalg.triangular_solve(a_mat + eye, jnp.broadcast_to(eye, a_mat.shape),
                                        left_side=True, lower=True)
    u = t_mat @ (vc * bc[..., None])
    w = t_mat @ (kbeta * jnp.exp(gc)[..., None])
    qk = jnp.einsum('nbhid,nbhjd->nbhij', qc, kc) * decay
    q_dec = qc * jnp.exp(gc)[..., None]
    k_dec = kc * jnp.exp(gc[..., -1:] - gc)[..., None]
    g_last = jnp.exp(gc[..., -1])

    def step(s, inp):
        u_i, w_i, qk_i, qd_i, kd_i, gl_i = inp
        v_new = u_i - jnp.einsum('bhck,bhkv->bhcv', w_i, s)
        o = jnp.einsum('bhck,bhkv->bhcv', qd_i, s) + jnp.einsum('bhij,bhjv->bhiv', qk_i, v_new)
        s = s * gl_i[..., None, None] + jnp.einsum('bhck,bhcv->bhkv', kd_i, v_new)
        return s, o

    s_fin, o = lax.scan(step, s0, (u, w, qk, q_dec, k_dec, g_last))
    o = jnp.moveaxis(o, (0, 2), (1, 3)).reshape(b, t, h, dv)
    return o, s_fin


def gdn_branch(hp, conv_prev, s0, conv_w, a_log, dt_bias, norm_g, chunk):
    b, t, _ = hp.shape
    qkv, conv_buf = causal_conv(hp[..., OFF_BQKV:OFF_BZ], conv_prev, conv_w)
    q = l2norm(qkv[..., :B_QK].reshape(b, t, B_HEADS, B_DK).astype(jnp.float32)) * (B_DK ** -0.5)
    k = l2norm(qkv[..., B_QK:2 * B_QK].reshape(b, t, B_HEADS, B_DK).astype(jnp.float32))
    v = qkv[..., 2 * B_QK:].reshape(b, t, B_HEADS, B_DV).astype(jnp.float32)
    z = hp[..., OFF_BZ:OFF_BB].reshape(b, t, B_HEADS, B_DV).astype(jnp.float32)
    beta = jax.nn.sigmoid(hp[..., OFF_BB:OFF_BA].astype(jnp.float32))
    g = -jnp.exp(a_log.astype(jnp.float32)) * jax.nn.softplus(
        hp[..., OFF_BA:OFF_CU].astype(jnp.float32) + dt_bias.astype(jnp.float32))
    o, s_fin = gated_delta_rule(q, k, v, g, beta, s0.astype(jnp.float32), chunk)
    o = o * lax.rsqrt(jnp.mean(o * o, -1, keepdims=True) + RMS_EPS) * norm_g * jax.nn.silu(z)
    return o.reshape(b, t, B_WIDTH).astype(hp.dtype), s_fin.astype(s0.dtype), conv_buf


def sgu_branch(hp, norm_g, norm_b, w_s, b_s):
    b, t, _ = hp.shape
    u = jax.nn.gelu(hp[..., OFF_CU:OFF_CV])
    v = layer_norm(jax.nn.gelu(hp[..., OFF_CV:OFF_G]), norm_g, norm_b)
    ln = min(t, C_CHUNK)
    n = t // ln
    idx = jnp.arange(ln)
    w = jnp.where(idx[:, None] >= idx[None, :], w_s[:, :ln, :ln], 0.0)
    vr = v.reshape(b, n, ln, C_GROUPS, C_GROUP_CH)
    mixed = jnp.einsum('gij,bnjgc->bnigc', w, vr) + b_s[:, :ln].T[None, None, :, :, None]
    return u * mixed.reshape(b, t, C_WIDTH), v


def cross_attn(x, mem_k, mem_v, w_q, w_o):
    b, t, _ = x.shape
    q = (x @ w_q).reshape(b, t, X_HEADS, X_HEAD_DIM)
    s = jnp.einsum('bthd,bmhd->bhtm', q, mem_k).astype(jnp.float32) * (X_HEAD_DIM ** -0.5)
    p = jax.nn.softmax(s, -1).astype(x.dtype)
    o = jnp.einsum('bhtm,bmhd->bthd', p, mem_v).reshape(b, t, X_HEADS * X_HEAD_DIM)
    return o @ w_o


def moe(x, w_r, b_r, w_gu, b_gu, w_d, b_d):
    b, t, d = x.shape
    xf = x.reshape(-1, d)
    n = xf.shape[0]
    logits = (xf @ w_r + b_r).astype(jnp.float32)
    top_v, top_i = lax.top_k(logits, TOP_K)
    gates = jax.nn.softmax(top_v, -1).astype(x.dtype)
    s = n * TOP_K
    flat_e = top_i.reshape(-1)
    order = jnp.argsort(flat_e)
    sorted_e = flat_e[order]
    counts = jnp.bincount(flat_e, length=N_EXPERTS)
    padded = (counts + MOE_BLOCK - 1) // MOE_BLOCK * MOE_BLOCK
    pad_end = jnp.cumsum(padded)
    pad_start = pad_end - padded
    start = jnp.cumsum(counts) - counts
    dest = pad_start[sorted_e] + jnp.arange(s) - start[sorted_e]
    n_blocks = -(-s // MOE_BLOCK) + N_EXPERTS
    rows = jnp.zeros((n_blocks * MOE_BLOCK,), jnp.int32).at[dest].set((order // TOP_K).astype(jnp.int32))
    blk_e = jnp.minimum(jnp.searchsorted(pad_end, jnp.arange(n_blocks) * MOE_BLOCK, side='right'),
                        N_EXPERTS - 1)
    xb = xf[rows].reshape(n_blocks, MOE_BLOCK, d)

    def expert_block(args):
        xblk, e = args
        hgu = xblk @ w_gu[e] + b_gu[e]
        gate = jnp.minimum(hgu[:, :D_FF], SWIGLU_LIMIT)
        up = jnp.clip(hgu[:, D_FF:], -SWIGLU_LIMIT, SWIGLU_LIMIT)
        act = (up + 1.0) * gate * jax.nn.sigmoid(gate * SWIGLU_ALPHA)
        return act @ w_d[e] + b_d[e]

    yb = lax.map(expert_block, (xb, blk_e)).reshape(-1, d)
    y_slots = jnp.zeros((s, d), yb.dtype).at[order].set(yb[dest])
    y = jnp.einsum('nkd,nk->nd', y_slots.reshape(n, TOP_K, d), gates)
    return y.reshape(b, t, d)


def trunk_layer(x, mem_k, mem_v, past_k, past_v, gdn_s0, conv_prev, q0, gdn_chunk, prm):
    (w_in, conv_w, a_log, dt_bias, gdn_g, sgu_g, sgu_bn, sgu_w, sgu_bias, w_oa, w_ob, w_oc, w_mo,
     w_xq, w_xo, w_r, b_r, w_gu, b_gu, w_d, b_d, ln_g, ln_b) = prm
    b, t, _ = x.shape
    hp = x @ w_in
    qa = hp[..., OFF_AQ:OFF_AK].reshape(b, t, A_HEADS, A_HEAD_DIM)
    ka = hp[..., OFF_AK:OFF_AV].reshape(b, t, A_HEADS, A_HEAD_DIM)
    va = hp[..., OFF_AV:OFF_BQKV].reshape(b, t, A_HEADS, A_HEAD_DIM)
    if past_k is None:
        k_all, v_all = ka, va
    else:
        k_all = jnp.concatenate([past_k, ka], axis=1)
        v_all = jnp.concatenate([past_v, va], axis=1)
    ya = moba_attention(qa, k_all, v_all, q0).reshape(b, t, A_WIDTH)
    yb, s_fin, conv_buf = gdn_branch(hp, conv_prev, gdn_s0, conv_w, a_log, dt_bias, gdn_g, gdn_chunk)
    yc, v_rows = sgu_branch(hp, sgu_g, sgu_bn, sgu_w, sgu_bias)
    gts = jax.nn.sigmoid(hp[..., OFF_G:].astype(jnp.float32)).astype(x.dtype).reshape(b, t, N_BRANCH, D_MODEL)
    merged = gts[..., 0, :] * (ya @ w_oa) + gts[..., 1, :] * (yb @ w_ob) + gts[..., 2, :] * (yc @ w_oc)
    x = layer_norm(DN_ALPHA * x + merged @ w_mo, ln_g[0], ln_b[0])
    x = layer_norm(DN_ALPHA * x + cross_attn(x, mem_k, mem_v, w_xq, w_xo), ln_g[1], ln_b[1])
    x = layer_norm(DN_ALPHA * x + moe(x, w_r, b_r, w_gu, b_gu, w_d, b_d), ln_g[2], ln_b[2])
    return x, ka, va, s_fin, conv_buf, v_rows


def setup_inputs(seed: int = 0) -> dict:
    key = jax.random.key(seed)
    keys = list(jax.random.split(key, 40))

    def nrm(shape, scale):
        return jax.random.normal(keys.pop(), shape, jnp.float32) * scale

    n_pages = PAST_LEN // PAGE_SIZE
    n_used = DEC_BATCH * n_pages
    n_pool = n_used + max(1, n_used // 4)
    x_prompt = nrm((BATCH, SEQ, D_MODEL), 1.0)
    x_sample = nrm((DEC_BATCH, DEC_SEQ, D_MODEL), 1.0)
    mem_prompt = nrm((BATCH, MEM_LEN, D_MODEL), 1.0)
    cache_moba_k = nrm((n_pool, DEPTH, PAGE_SIZE, A_HEADS, A_HEAD_DIM), 1.0)
    cache_moba_v = nrm((n_pool, DEPTH, PAGE_SIZE, A_HEADS, A_HEAD_DIM), 1.0)
    page_table = jax.random.permutation(keys.pop(), n_pool)[:n_used].reshape(DEC_BATCH, n_pages).astype(jnp.int32)
    state_gdn = nrm((DEC_BATCH, DEPTH, B_HEADS, B_DK, B_DV), 0.1)
    state_gdn_conv = nrm((DEC_BATCH, DEPTH, CONV_W - 1, B_CONV_CH), 1.0)
    cache_mem_k = nrm((DEC_BATCH, DEPTH, MEM_LEN, X_HEADS, X_HEAD_DIM), 1.0)
    cache_mem_v = nrm((DEC_BATCH, DEPTH, MEM_LEN, X_HEADS, X_HEAD_DIM), 1.0)
    w_in = nrm((DEPTH, D_MODEL, P_IN), D_MODEL ** -0.5)
    gdn_conv_w = nrm((DEPTH, CONV_W, B_CONV_CH), CONV_W ** -0.5)
    gdn_a_log = jnp.log(jax.random.uniform(keys.pop(), (DEPTH, B_HEADS), jnp.float32, 1.0, 16.0))
    dt = jnp.exp(jax.random.uniform(keys.pop(), (DEPTH, B_HEADS), jnp.float32, math.log(1e-3), math.log(1e-1)))
    gdn_dt_bias = dt + jnp.log(-jnp.expm1(-dt))
    gdn_norm_g = 1.0 + nrm((DEPTH, B_DV), 0.02)
    sgu_norm_g = 1.0 + nrm((DEPTH, C_WIDTH), 0.02)
    sgu_norm_b = nrm((DEPTH, C_WIDTH), 0.02)
    sgu_w = nrm((DEPTH, C_GROUPS, C_CHUNK, C_CHUNK), C_CHUNK ** -0.5)
    sgu_b = 1.0 + nrm((DEPTH, C_GROUPS, C_CHUNK), 0.02)
    w_out_a = nrm((DEPTH, A_WIDTH, D_MODEL), A_WIDTH ** -0.5)
    w_out_b = nrm((DEPTH, B_WIDTH, D_MODEL), B_WIDTH ** -0.5)
    w_out_c = nrm((DEPTH, C_WIDTH, D_MODEL), C_WIDTH ** -0.5)
    w_mix_out = nrm((DEPTH, D_MODEL, D_MODEL), D_MODEL ** -0.5 * DN_BETA)
    w_xq = nrm((DEPTH, D_MODEL, X_HEADS * X_HEAD_DIM), D_MODEL ** -0.5)
    w_xk = nrm((DEPTH, D_MODEL, X_HEADS * X_HEAD_DIM), D_MODEL ** -0.5)
    w_xv = nrm((DEPTH, D_MODEL, X_HEADS * X_HEAD_DIM), D_MODEL ** -0.5)
    w_xo = nrm((DEPTH, X_HEADS * X_HEAD_DIM, D_MODEL), (X_HEADS * X_HEAD_DIM) ** -0.5 * DN_BETA)
    w_router = nrm((DEPTH, D_MODEL, N_EXPERTS), D_MODEL ** -0.5)
    b_router = nrm((DEPTH, N_EXPERTS), 0.01)
    w_gate_up = nrm((DEPTH, N_EXPERTS, D_MODEL, 2 * D_FF), D_MODEL ** -0.5)
    b_gate_up = nrm((DEPTH, N_EXPERTS, 2 * D_FF), 0.02)
    w_down = nrm((DEPTH, N_EXPERTS, D_FF, D_MODEL), D_FF ** -0.5 * DN_BETA)
    b_down = nrm((DEPTH, N_EXPERTS, D_MODEL), 0.02)
    ln_g = 1.0 + nrm((DEPTH, 3, D_MODEL), 0.02)
    ln_b = nrm((DEPTH, 3, D_MODEL), 0.02)
    return {'x_prompt': x_prompt, 'x_sample': x_sample, 'mem_prompt': mem_prompt,
            'cache_moba_k': cache_moba_k, 'cache_moba_v': cache_moba_v, 'page_table': page_table,
            'state_gdn': state_gdn, 'state_gdn_conv': state_gdn_conv,
            'cache_mem_k': cache_mem_k, 'cache_mem_v': cache_mem_v,
            'w_in': w_in, 'gdn_conv_w': gdn_conv_w, 'gdn_a_log': gdn_a_log, 'gdn_dt_bias': gdn_dt_bias,
            'gdn_norm_g': gdn_norm_g, 'sgu_norm_g': sgu_norm_g, 'sgu_norm_b': sgu_norm_b,
            'sgu_w': sgu_w, 'sgu_b': sgu_b, 'w_out_a': w_out_a, 'w_out_b': w_out_b, 'w_out_c': w_out_c,
            'w_mix_out': w_mix_out, 'w_xq': w_xq, 'w_xk': w_xk, 'w_xv': w_xv, 'w_xo': w_xo,
            'w_router': w_router, 'b_router': b_router, 'w_gate_up': w_gate_up, 'b_gate_up': b_gate_up,
            'w_down': w_down, 'b_down': b_down, 'ln_g': ln_g, 'ln_b': ln_b}


def reference(x_prompt, x_sample, mem_prompt, cache_moba_k, cache_moba_v, page_table, state_gdn,
              state_gdn_conv, cache_mem_k, cache_mem_v, w_in, gdn_conv_w, gdn_a_log, gdn_dt_bias,
              gdn_norm_g, sgu_norm_g, sgu_norm_b, sgu_w, sgu_b, w_out_a, w_out_b, w_out_c, w_mix_out,
              w_xq, w_xk, w_xv, w_xo, w_router, b_router, w_gate_up, b_gate_up, w_down, b_down,
              ln_g, ln_b):
    bp = x_prompt.shape[0]
    db, t_s, _ = x_sample.shape
    n_pages = page_table.shape[1]
    past_len = n_pages * cache_moba_k.shape[2]
    yp, ys = x_prompt, x_sample
    kp_l, vp_l, ks_l, vs_l, sp_l, ss_l, cp_l, cs_l, mk_l, mv_l, vr_l = ([] for _ in range(11))
    for l in range(DEPTH):
        prm = (w_in[l], gdn_conv_w[l], gdn_a_log[l], gdn_dt_bias[l], gdn_norm_g[l], sgu_norm_g[l],
               sgu_norm_b[l], sgu_w[l], sgu_b[l], w_out_a[l], w_out_b[l], w_out_c[l], w_mix_out[l],
               w_xq[l], w_xo[l], w_router[l], b_router[l], w_gate_up[l], b_gate_up[l], w_down[l],
               b_down[l], ln_g[l], ln_b[l])
        mk = (mem_prompt @ w_xk[l]).reshape(bp, -1, X_HEADS, X_HEAD_DIM)
        mv = (mem_prompt @ w_xv[l]).reshape(bp, -1, X_HEADS, X_HEAD_DIM)
        conv0 = jnp.zeros((bp, CONV_W - 1, B_CONV_CH), x_prompt.dtype)
        s0 = jnp.zeros((bp, B_HEADS, B_DK, B_DV), x_prompt.dtype)
        yp, kp, vp, sp, cp, _ = trunk_layer(yp, mk, mv, None, None, s0, conv0, 0, GDN_CHUNK, prm)
        past_k = cache_moba_k[page_table, l].reshape(db, past_len, A_HEADS, A_HEAD_DIM)
        past_v = cache_moba_v[page_table, l].reshape(db, past_len, A_HEADS, A_HEAD_DIM)
        ys, ks, vs, ss, cs, vr = trunk_layer(ys, cache_mem_k[:, l], cache_mem_v[:, l], past_k, past_v,
                                             state_gdn[:, l], state_gdn_conv[:, l], past_len, t_s, prm)
        kp_l.append(kp); vp_l.append(vp); ks_l.append(ks); vs_l.append(vs)
        sp_l.append(sp); ss_l.append(ss); cp_l.append(cp); cs_l.append(cs)
        mk_l.append(mk); mv_l.append(mv); vr_l.append(vr)
    moba_k_prompt = jnp.stack(kp_l, axis=1)
    moba_v_prompt = jnp.stack(vp_l, axis=1)
    moba_k_sample = jnp.stack(ks_l, axis=1)
    moba_v_sample = jnp.stack(vs_l, axis=1)
    gdn_state_prompt = jnp.stack(sp_l, axis=1)
    gdn_state_sample = jnp.stack(ss_l, axis=1)
    gdn_conv_prompt = jnp.stack(cp_l, axis=1)
    gdn_conv_sample = jnp.stack(cs_l, axis=1)
    mem_k_prompt = jnp.stack(mk_l, axis=1)
    mem_v_prompt = jnp.stack(mv_l, axis=1)
    sgu_v_sample = jnp.stack(vr_l, axis=1)
    return (yp, ys, moba_k_prompt, moba_v_prompt, moba_k_sample, moba_v_sample, gdn_state_prompt,
            gdn_state_sample, gdn_conv_prompt, gdn_conv_sample, mem_k_prompt, mem_v_prompt, sgu_v_sample)
```

```python
import functools

import jax
import jax.numpy as jnp
from jax import lax
from jax.experimental import pallas as pl
from jax.experimental.pallas import tpu as pltpu

F32 = jnp.float32
BF16 = jnp.bfloat16

D_MODEL = 1024
DEPTH = 2
A_HEADS = 8
A_HEAD_DIM = 64
A_WIDTH = 512
MOBA_BLOCK = 256
MOBA_TOPK = 3
Q_BLOCK = 128
B_HEADS = 8
B_DK = 64
B_WIDTH = 512
B_CONV_CH = 1536
CONV_W = 4
GDN_CHUNK = 64
C_GROUPS = 4
C_CHUNK = 128
C_WIDTH = 512
X_HEADS = 4
X_HEAD_DIM = 256
N_EXPERTS = 32
TOP_K = 4
D_FF = 1024
SWIGLU_LIMIT = 7.0
SWIGLU_ALPHA = 1.702
DN_ALPHA = (2 * DEPTH) ** 0.25
LN_EPS = 1e-5
RMS_EPS = 1e-6

OFF_BZ = 3072
OFF_BB = 3584
OFF_CU = 3600
P_IN = 7696
W_A = 3584
W_C = P_IN - OFF_CU
LANES = 128
MOE_BLK = 256
NEG = -1e30
VMEM_LIMIT = 56 * 1024 * 1024


def _cp(sem, vmem=VMEM_LIMIT):
    return pltpu.CompilerParams(dimension_semantics=sem, vmem_limit_bytes=vmem)


def _bf(x):
    return x.astype(BF16)


def _dot(a, b):
    return jnp.dot(_bf(a), _bf(b), preferred_element_type=F32)


def _dot_nt(a, b):
    return lax.dot_general(_bf(a), _bf(b), (((1,), (1,)), ((), ())), preferred_element_type=F32)


def _split(a, n):
    parts, r = [], a
    for _ in range(n):
        p = r.astype(BF16)
        parts.append(p)
        r = r - p.astype(F32)
    return parts


def _dot_x(a, b_bf, n):
    acc = None
    for p in _split(a, n):
        d = jnp.dot(p, b_bf, preferred_element_type=F32)
        acc = d if acc is None else acc + d
    return acc


def _dot_nt_hl(a, b):
    ah, al = _split(a, 2)
    bh, bl = _split(b, 2)
    dn = (((1,), (1,)), ((), ()))
    f = lambda x, y: lax.dot_general(x, y, dn, preferred_element_type=F32)
    return f(ah, bh) + f(ah, bl) + f(al, bh)


def _dot_hl(a, b):
    ah, al = _split(a, 2)
    bh, bl = _split(b, 2)
    f = lambda x, y: jnp.dot(x, y, preferred_element_type=F32)
    return f(ah, bh) + f(ah, bl) + f(al, bh)


def _iota(shape, dim):
    return lax.broadcasted_iota(jnp.int32, shape, dim)


def _ln(x, g, b):
    mu = jnp.mean(x, axis=-1, keepdims=True)
    xc = x - mu
    var = jnp.mean(xc * xc, axis=-1, keepdims=True)
    return xc * lax.rsqrt(var + LN_EPS) * g + b


def _softplus(x):
    return jnp.maximum(x, 0.0) + jnp.log(1.0 + jnp.exp(-jnp.abs(x)))


def _top_mask(g, idx, n_idx, k):
    sel = jnp.zeros(g.shape, F32)
    for _ in range(k):
        m = jnp.max(g, axis=-1, keepdims=True)
        first = jnp.min(jnp.where(g == m, idx, float(n_idx)), axis=-1, keepdims=True)
        pick = (idx == first) & (m > -jnp.inf)
        sel = jnp.where(pick, 1.0, sel)
        g = jnp.where(pick, -jnp.inf, g)
    return sel


def _mm_kernel(x_ref, w_ref, o_ref, xb_ref):
    @pl.when(pl.program_id(1) == 0)
    def _():
        xb_ref[...] = _bf(x_ref[...])
    o_ref[...] = jnp.dot(xb_ref[...], w_ref[...], preferred_element_type=F32)


def _matmul(x, w_bf, tm, tn):
    m, k = x.shape
    n = w_bf.shape[1]
    tm, tn = min(tm, m), min(tn, n)
    return pl.pallas_call(
        _mm_kernel, grid=(m // tm, n // tn),
        in_specs=[pl.BlockSpec((tm, k), lambda i, j: (i, 0)),
                  pl.BlockSpec((k, tn), lambda i, j: (0, j))],
        out_specs=pl.BlockSpec((tm, tn), lambda i, j: (i, j)),
        out_shape=jax.ShapeDtypeStruct((m, n), F32),
        scratch_shapes=[pltpu.VMEM((tm, k), BF16)],
        compiler_params=_cp(("parallel", "arbitrary")), name="matmul")(x, w_bf)


def _proj_ln_kernel(a_ref, w_ref, r_ref, g_ref, b_ref, o_ref):
    y = jnp.dot(_bf(a_ref[...]), w_ref[...], preferred_element_type=F32)
    o_ref[...] = _ln(DN_ALPHA * r_ref[...] + y, g_ref[...], b_ref[...])


def _proj_res_ln(a, w_bf, res, g, b, tm):
    m, k = a.shape
    tm = min(tm, m)
    row = lambda i: (i, 0)
    fix = lambda i: (0, 0)
    return pl.pallas_call(
        _proj_ln_kernel, grid=(m // tm,),
        in_specs=[pl.BlockSpec((tm, k), row), pl.BlockSpec((k, D_MODEL), fix),
                  pl.BlockSpec((tm, D_MODEL), row), pl.BlockSpec((1, D_MODEL), fix),
                  pl.BlockSpec((1, D_MODEL), fix)],
        out_specs=pl.BlockSpec((tm, D_MODEL), row),
        out_shape=jax.ShapeDtypeStruct((m, D_MODEL), F32),
        compiler_params=_cp(("parallel",)), name="proj_res_ln")(a, w_bf, res, g, b)


def _gdn_pre(c, hs, bd, eb, eg, alog, dtb):
    q = c[:, :512]
    k = c[:, 512:1024]
    v = c[:, 1024:]
    qn = q * lax.rsqrt(_dot_x(q * q, bd, 2) + RMS_EPS) * (B_DK ** -0.5)
    kn = k * lax.rsqrt(_dot_x(k * k, bd, 2) + RMS_EPS)
    beta = jax.nn.sigmoid(hs)
    g = -jnp.exp(alog) * _softplus(hs + dtb)
    return qn, kn, v, _dot_x(beta, eb, 3), _dot_x(g, eg, 3)


def _post_p_kernel(bqkv_ref, hs_ref, cuv_ref, cw_ref, bd_ref, eb_ref, eg_ref, alog_ref, dtb_ref,
                   sg_ref, sb_ref, ws_ref, bx_ref,
                   qn_ref, kn_ref, v_ref, beta_ref, g_ref, yc_ref, xe_ref, *, tm):
    @pl.when(pl.program_id(1) == 0)
    def _():
        xe_ref[0:8, :] = jnp.zeros((8, B_CONV_CH), F32)
    xin = bqkv_ref[...]
    xe_ref[8:8 + tm, :] = xin
    cw = cw_ref[...]
    acc = xe_ref[5:5 + tm, :] * cw[0:1]
    acc = acc + xe_ref[6:6 + tm, :] * cw[1:2]
    acc = acc + xe_ref[7:7 + tm, :] * cw[2:3]
    acc = acc + xin * cw[3:4]
    xe_ref[0:8, :] = xe_ref[tm:tm + 8, :]
    c = acc * jax.nn.sigmoid(acc)
    qn, kn, v, bx, gx = _gdn_pre(c, hs_ref[...], bd_ref[...], eb_ref[...], eg_ref[...],
                                 alog_ref[...], dtb_ref[...])
    qn_ref[...] = qn
    kn_ref[...] = kn
    v_ref[...] = v
    beta_ref[...] = bx
    g_ref[...] = gx
    u = jax.nn.gelu(cuv_ref[:, :C_WIDTH])
    vr = _bf(_ln(jax.nn.gelu(cuv_ref[:, C_WIDTH:]), sg_ref[...], sb_ref[...]))
    for ch in range(tm // C_CHUNK):
        rs = slice(ch * C_CHUNK, (ch + 1) * C_CHUNK)
        for gi in range(C_GROUPS):
            cs = slice(gi * LANES, (gi + 1) * LANES)
            mixed = jnp.dot(ws_ref[gi], vr[rs, cs], preferred_element_type=F32) + bx_ref[:, cs]
            yc_ref[rs, cs] = u[rs, cs] * mixed


def _post_prompt(hpa, hps, hpc, consts, sgu, bsz, seq, tm=256):
    n = bsz * seq
    nt = seq // tm
    row = lambda b, t: (b * nt + t, 0)
    fix2 = lambda b, t: (0, 0)
    fix3 = lambda b, t: (0, 0, 0)
    cw, bd, eb, eg, alog, dtb = consts
    sg, sb, ws, bx = sgu
    o512 = jax.ShapeDtypeStruct((n, 512), F32)
    return pl.pallas_call(
        functools.partial(_post_p_kernel, tm=tm), grid=(bsz, nt),
        in_specs=[pl.BlockSpec((tm, B_CONV_CH), lambda b, t: (b * nt + t, 1)),
                  pl.BlockSpec((tm, LANES), row),
                  pl.BlockSpec((tm, 2 * C_WIDTH), row),
                  pl.BlockSpec((CONV_W, B_CONV_CH), fix2),
                  pl.BlockSpec((512, 512), fix2), pl.BlockSpec((LANES, 512), fix2),
                  pl.BlockSpec((LANES, 512), fix2), pl.BlockSpec((1, LANES), fix2),
                  pl.BlockSpec((1, LANES), fix2),
                  pl.BlockSpec((1, C_WIDTH), fix2), pl.BlockSpec((1, C_WIDTH), fix2),
                  pl.BlockSpec((C_GROUPS, C_CHUNK, C_CHUNK), fix3),
                  pl.BlockSpec((C_CHUNK, C_WIDTH), fix2)],
        out_specs=[pl.BlockSpec((tm, 512), row)] * 6,
        out_shape=[o512] * 6,
        scratch_shapes=[pltpu.VMEM((tm + 8, B_CONV_CH), F32)],
        compiler_params=_cp(("parallel", "arbitrary")), name="post_prompt",
    )(hpa, hps, hpc, cw, bd, eb, eg, alog, dtb, sg, sb, ws, bx)


def _post_s_kernel(hpa_ref, hps_ref, hpc_ref, prev_ref, cw_ref, bd_ref, eb_ref, eg_ref, alog_ref,
                   dtb_ref, sg_ref, sb_ref, wr_ref, br_ref,
                   qn_ref, kn_ref, v_ref, beta_ref, g_ref, yc_ref, vr_ref, *, ts):
    cw = cw_ref[...]
    xe = [prev_ref[:, j * B_CONV_CH:(j + 1) * B_CONV_CH] for j in range(CONV_W - 1)]
    xe += [hpa_ref[:, p * W_A + 1536:p * W_A + 3072] for p in range(ts)]
    vrows = []
    for p in range(ts):
        acc = xe[p] * cw[0:1]
        for j in range(1, CONV_W):
            acc = acc + xe[p + j] * cw[j:j + 1]
        c = acc * jax.nn.sigmoid(acc)
        hs = hps_ref[:, p * LANES:(p + 1) * LANES]
        qn, kn, v, bx, gx = _gdn_pre(c, hs, bd_ref[...], eb_ref[...], eg_ref[...],
                                     alog_ref[...], dtb_ref[...])
        ps = slice(p * 512, (p + 1) * 512)
        qn_ref[:, ps] = qn
        kn_ref[:, ps] = kn
        v_ref[:, ps] = v
        beta_ref[:, ps] = bx
        g_ref[:, ps] = gx
        vr = _ln(jax.nn.gelu(hpc_ref[:, p * W_C + C_WIDTH:p * W_C + 2 * C_WIDTH]),
                 sg_ref[...], sb_ref[...])
        vr_ref[:, ps] = vr
        vrows.append(vr)
        mixed = br_ref[p:p + 1, :]
        for j in range(p + 1):
            mixed = mixed + wr_ref[p * ts + j:p * ts + j + 1, :] * vrows[j]
        yc_ref[:, ps] = jax.nn.gelu(hpc_ref[:, p * W_C:p * W_C + C_WIDTH]) * mixed


def _post_sample(hpa, hps, hpc, prev, consts, sgu, db, ts):
    cw, bd, eb, eg, alog, dtb = consts
    sg, sb, wr, br = sgu
    args = (hpa.reshape(db, ts * W_A), hps.reshape(db, ts * LANES), hpc.reshape(db, ts * W_C),
            prev.reshape(db, (CONV_W - 1) * B_CONV_CH), cw, bd, eb, eg, alog, dtb, sg, sb, wr, br)
    o = jax.ShapeDtypeStruct((db, ts * 512), F32)
    return pl.pallas_call(
        functools.partial(_post_s_kernel, ts=ts),
        out_shape=[o] * 7, compiler_params=_cp(None), name="post_sample")(*args)


def _gdn_kernel(q_ref, k_ref, v_ref, bx_ref, gx_ref, z_ref, s0_ref, ng_ref, o_ref, sf_ref, s_scr,
                *, c_len, bb, n_dbl):
    ci = pl.program_id(1)

    @pl.when(ci == 0)
    def _():
        for i in range(bb):
            for h in range(B_HEADS):
                s_scr[i * B_HEADS + h] = s0_ref[i, h]

    ri = _iota((c_len, c_len), 0)
    cj = _iota((c_len, c_len), 1)
    low = ri >= cj
    strict = ri > cj
    tril_bf = jnp.where(low, 1.0, 0.0).astype(BF16)
    eye_c = jnp.where(ri == cj, 1.0, 0.0)
    eye_dk = jnp.where(_iota((B_DK, B_DK), 0) == _iota((B_DK, B_DK), 1), 1.0, 0.0).astype(BF16)
    e0 = jnp.where(_iota((c_len, B_DK), 1) == 0, 1.0, 0.0).astype(BF16)
    ng = ng_ref[...]
    for i in range(bb):
        rs = slice(i * c_len, (i + 1) * c_len)
        for h in range(B_HEADS):
            cs = slice(h * B_DK, (h + 1) * B_DK)
            q, k, v = q_ref[rs, cs], k_ref[rs, cs], v_ref[rs, cs]
            bx, gx, z = bx_ref[rs, cs], gx_ref[rs, cs], z_ref[rs, cs]
            gcb = None
            for part in _split(gx, 3):
                d = jnp.dot(tril_bf, part, preferred_element_type=F32)
                gcb = d if gcb is None else gcb + d
            gcj = None
            for part in _split(gcb, 3):
                d = lax.dot_general(e0, part, (((1,), (1,)), ((), ())), preferred_element_type=F32)
                gcj = d if gcj is None else gcj + d
            gci = gcb[:, :c_len]
            decay = jnp.where(low, jnp.exp(jnp.where(low, gci - gcj, 0.0)), 0.0)
            eg = jnp.exp(gcb)
            kb = k * bx
            a_mat = jnp.where(strict, _dot_nt(kb, k) * decay, 0.0)
            t_mat = eye_c - a_mat
            pw = a_mat
            for _ in range(n_dbl):
                pw = _dot(pw, pw)
                t_mat = t_mat + _dot(t_mat, pw)
            u = _dot(t_mat, v * bx)
            w = _dot(t_mat, kb * eg)
            qk = _dot_nt(q, k) * decay
            g_last = gcb[c_len - 1:c_len, :]
            kd = k * jnp.exp(g_last - gcb)
            s = s_scr[i * B_HEADS + h]
            v_new = u - _dot(w, s)
            o = _dot(q * eg, s) + _dot(qk, v_new)
            kd_t = lax.dot_general(eye_dk, _bf(kd), (((1,), (1,)), ((), ())),
                                   preferred_element_type=F32)
            s_scr[i * B_HEADS + h] = s * jnp.exp(g_last) + _dot(kd_t, v_new)
            o = o * lax.rsqrt(jnp.mean(o * o, axis=-1, keepdims=True) + RMS_EPS) * ng
            o_ref[rs, cs] = o * (z * jax.nn.sigmoid(z))

    @pl.when(ci == pl.num_programs(1) - 1)
    def _():
        for i in range(bb):
            for h in range(B_HEADS):
                sf_ref[i, h] = s_scr[i * B_HEADS + h]


def _gdn(qn, kn, v, bx, gx, z_src, z_col, s0, ng, bsz, rows_per_b, c_len, bb):
    nc = rows_per_b // c_len
    n_dbl = max(c_len.bit_length() - 2, 0)
    tile = bb * c_len
    row = lambda b, c: (b * nc + c, 0)
    in512 = pl.BlockSpec((tile, 512), row)
    st = pl.BlockSpec((bb, B_HEADS, B_DK, B_DK), lambda b, c: (b, 0, 0, 0))
    return pl.pallas_call(
        functools.partial(_gdn_kernel, c_len=c_len, bb=bb, n_dbl=n_dbl),
        grid=(bsz // bb, nc),
        in_specs=[in512] * 5 + [pl.BlockSpec((tile, 512), lambda b, c: (b * nc + c, z_col)), st,
                                pl.BlockSpec((1, B_DK), lambda b, c: (0, 0))],
        out_specs=[in512, st],
        out_shape=[jax.ShapeDtypeStruct((bsz * rows_per_b, 512), F32),
                   jax.ShapeDtypeStruct((bsz, B_HEADS, B_DK, B_DK), F32)],
        scratch_shapes=[pltpu.VMEM((bb * B_HEADS, B_DK, B_DK), F32)],
        compiler_params=_cp(("parallel", "arbitrary")), name="gdn",
    )(qn, kn, v, bx, gx, z_src, s0, ng)


def _moba_p_kernel(q_ref, k_ref, v_ref, o_ref, km_ref, *, nb, nbp):
    qi = pl.program_id(2)

    @pl.when(qi == 0)
    def _():
        km_ref[...] = jnp.zeros((nbp, LANES), F32)
        for n in range(nb):
            km_ref[n:n + 1, :] = jnp.mean(k_ref[0, n * MOBA_BLOCK:(n + 1) * MOBA_BLOCK, :],
                                          axis=0, keepdims=True)

    own = qi // (MOBA_BLOCK // Q_BLOCK)
    q2 = q_ref[0]
    lane = _iota((Q_BLOCK, LANES), 1)
    first_head = lane < A_HEAD_DIM
    qh = [jnp.where(first_head, q2, 0.0), jnp.where(first_head, 0.0, q2)]
    qhb = [_bf(x) for x in qh]
    km = km_ref[...]
    blk = _iota((Q_BLOCK, nbp), 1)
    blk_f = blk.astype(F32)
    sels = []
    for h in range(2):
        g = _dot_nt_hl(qh[h], km)
        g = jnp.where(blk < own, g, -jnp.inf)
        sels.append(_top_mask(g, blk_f, nbp, MOBA_TOPK))

    start = pl.multiple_of(own * MOBA_BLOCK, MOBA_BLOCK)
    k_own = _bf(k_ref[0, pl.ds(start, MOBA_BLOCK), :])
    v_own = _bf(v_ref[0, pl.ds(start, MOBA_BLOCK), :])
    kpos = own * MOBA_BLOCK + _iota((Q_BLOCK, MOBA_BLOCK), 1)
    qpos = qi * Q_BLOCK + _iota((Q_BLOCK, MOBA_BLOCK), 0)
    causal = kpos <= qpos
    scale = A_HEAD_DIM ** -0.5
    init = []
    for h in range(2):
        s = lax.dot_general(qhb[h], k_own, (((1,), (1,)), ((), ())), preferred_element_type=F32)
        s = jnp.where(causal, s * scale, NEG)
        m = jnp.max(s, axis=-1, keepdims=True)
        p = jnp.exp(s - m)
        init += [m, jnp.sum(p, axis=-1, keepdims=True),
                 jnp.dot(_bf(p), v_own, preferred_element_type=F32)]

    def body(j, carry):
        st = pl.multiple_of(j * MOBA_BLOCK, MOBA_BLOCK)
        kj = _bf(k_ref[0, pl.ds(st, MOBA_BLOCK), :])
        vj = _bf(v_ref[0, pl.ds(st, MOBA_BLOCK), :])
        out = []
        for h in range(2):
            m, l, acc = carry[3 * h:3 * h + 3]
            selj = jnp.sum(jnp.where(blk == j, sels[h], 0.0), axis=-1, keepdims=True)
            s = lax.dot_general(qhb[h], kj, (((1,), (1,)), ((), ())), preferred_element_type=F32)
            s = jnp.where(selj > 0.0, s * scale, NEG)
            m_new = jnp.maximum(m, jnp.max(s, axis=-1, keepdims=True))
            alpha = jnp.exp(m - m_new)
            p = jnp.exp(s - m_new)
            l = l * alpha + jnp.sum(p, axis=-1, keepdims=True)
            acc = acc * alpha + jnp.dot(_bf(p), vj, preferred_element_type=F32)
            out += [m_new, l, acc]
        return tuple(out)

    m0, l0, a0, m1, l1, a1 = lax.fori_loop(0, own, body, tuple(init))
    o_ref[0] = jnp.where(first_head, a0 / l0, a1 / l1)


def _moba_prompt(hpa3, bsz, seq):
    nb = seq // MOBA_BLOCK
    nbp = max(8, -(-nb // 8) * 8)
    npair = A_WIDTH // LANES
    return pl.pallas_call(
        functools.partial(_moba_p_kernel, nb=nb, nbp=nbp),
        grid=(bsz, npair, seq // Q_BLOCK),
        in_specs=[pl.BlockSpec((1, Q_BLOCK, LANES), lambda b, p, i: (b, i, p)),
                  pl.BlockSpec((1, seq, LANES), lambda b, p, i: (b, 0, npair + p)),
                  pl.BlockSpec((1, seq, LANES), lambda b, p, i: (b, 0, 2 * npair + p))],
        out_specs=pl.BlockSpec((1, Q_BLOCK, LANES), lambda b, p, i: (b, i, p)),
        out_shape=jax.ShapeDtypeStruct((bsz, seq, A_WIDTH), F32),
        scratch_shapes=[pltpu.VMEM((nbp, LANES), F32)],
        compiler_params=_cp(("parallel", "parallel", "arbitrary")), name="moba_prompt",
    )(hpa3, hpa3, hpa3)


def _moba_s_kernel(pt_ref, q_ref, kn_ref, vn_ref, *refs, n_pages, ts):
    k_pages = refs[:n_pages]
    v_pages = refs[n_pages:2 * n_pages]
    o_ref = refs[2 * n_pages]
    km_ref, s_ref = refs[2 * n_pages + 1:]
    ppb = MOBA_BLOCK // k_pages[0].shape[2]
    page = k_pages[0].shape[2]
    nb = n_pages // ppb
    rows = A_HEADS * 8
    q8 = q_ref[0]
    lane_head = _iota((8, A_WIDTH), 1) // A_HEAD_DIM
    qbd = jnp.concatenate([jnp.where(lane_head == h, q8, 0.0) for h in range(A_HEADS)], axis=0)
    qbd_b = _bf(qbd)
    scale = A_HEAD_DIM ** -0.5
    nt = (((1,), (1,)), ((), ()))
    km_ref[...] = jnp.zeros(km_ref.shape, F32)
    for n in range(nb):
        tot = None
        for pp in range(ppb):
            sm = jnp.sum(k_pages[n * ppb + pp][0, 0], axis=0, keepdims=True)
            tot = sm if tot is None else tot + sm
        km_ref[n:n + 1, :] = tot * (1.0 / MOBA_BLOCK)
    nbp = km_ref.shape[0]
    blk = _iota((rows, nbp), 1)
    g = _dot_nt_hl(qbd, km_ref[...])
    g = jnp.where(blk < nb, g, -jnp.inf)
    sel = _top_mask(g, blk.astype(F32), nbp, min(MOBA_TOPK, nb))
    s_own = lax.dot_general(qbd_b, _bf(kn_ref[0]), nt, preferred_element_type=F32) * scale
    kr = _iota((rows, 8), 1)
    qr = _iota((rows, 8), 0) % 8
    s_own = jnp.where((kr <= qr) & (kr < ts), s_own, NEG)
    m = jnp.max(s_own, axis=-1, keepdims=True)
    for pg in range(n_pages):
        s = lax.dot_general(qbd_b, _bf(k_pages[pg][0, 0]), nt, preferred_element_type=F32) * scale
        s = jnp.where(sel[:, pg // ppb:pg // ppb + 1] > 0.0, s, NEG)
        s_ref[:, pg * page:(pg + 1) * page] = s
        m = jnp.maximum(m, jnp.max(s, axis=-1, keepdims=True))
    p_own = jnp.exp(s_own - m)
    l = jnp.sum(p_own, axis=-1, keepdims=True)
    for pg in range(n_pages):
        p = jnp.exp(s_ref[:, pg * page:(pg + 1) * page] - m)
        s_ref[:, pg * page:(pg + 1) * page] = p
        l = l + jnp.sum(p, axis=-1, keepdims=True)
    inv = 1.0 / l
    o = jnp.dot(_bf(p_own * inv), _bf(vn_ref[0]), preferred_element_type=F32)
    for pg in range(n_pages):
        p = s_ref[:, pg * page:(pg + 1) * page] * inv
        o = o + jnp.dot(_bf(p), _bf(v_pages[pg][0, 0]), preferred_element_type=F32)
    out = jnp.zeros((8, A_WIDTH), F32)
    for h in range(A_HEADS):
        out = out + jnp.where(lane_head == h, o[h * 8:(h + 1) * 8, :], 0.0)
    o_ref[0] = out


def _moba_sample(q8, kn8, vn8, cache_k, cache_v, pt_flat, layer, db, n_pages, ts):
    page = cache_k.shape[2]
    nb = n_pages * page // MOBA_BLOCK
    nbp = max(8, -(-nb // 8) * 8)
    tok = pl.BlockSpec((1, 8, A_WIDTH), lambda b, pt: (b, 0, 0))

    def page_spec(p):
        return pl.BlockSpec((1, 1, page, A_WIDTH), lambda b, pt: (pt[b * n_pages + p], layer, 0, 0))

    specs = [page_spec(p) for p in range(n_pages)]
    grid_spec = pltpu.PrefetchScalarGridSpec(
        num_scalar_prefetch=1, grid=(db,),
        in_specs=[tok, tok, tok] + specs + specs,
        out_specs=tok,
        scratch_shapes=[pltpu.VMEM((nbp, A_WIDTH), F32),
                        pltpu.VMEM((A_HEADS * 8, n_pages * page), F32)])
    return pl.pallas_call(
        functools.partial(_moba_s_kernel, n_pages=n_pages, ts=ts),
        grid_spec=grid_spec, out_shape=jax.ShapeDtypeStruct((db, 8, A_WIDTH), F32),
        compiler_params=_cp(("parallel",)), name="moba_sample",
    )(pt_flat, q8, kn8, vn8, *([cache_k] * n_pages), *([cache_v] * n_pages))


def _merge_kernel(ya_ref, yb_ref, yc_ref, g0_ref, g1_ref, g2_ref, x_ref, woa_ref, wob_ref, woc_ref,
                  wmo_ref, lg_ref, lb_ref, o_ref):
    f = lambda y, w: jnp.dot(_bf(y[...]), w[...], preferred_element_type=F32)
    merged = jax.nn.sigmoid(g0_ref[...]) * f(ya_ref, woa_ref)
    merged = merged + jax.nn.sigmoid(g1_ref[...]) * f(yb_ref, wob_ref)
    merged = merged + jax.nn.sigmoid(g2_ref[...]) * f(yc_ref, woc_ref)
    y = jnp.dot(_bf(merged), wmo_ref[...], preferred_element_type=F32)
    o_ref[...] = _ln(DN_ALPHA * x_ref[...] + y, lg_ref[...], lb_ref[...])


def _merge(ya, yb, yc, hpc, x, woa, wob, woc, wmo, lg, lb, tm=256):
    m = x.shape[0]
    row = lambda i: (i, 0)
    fix = lambda i: (0, 0)
    y512 = pl.BlockSpec((tm, 512), row)
    gate = lambda j: pl.BlockSpec((tm, D_MODEL), lambda i: (i, j))
    wsm = pl.BlockSpec((512, D_MODEL), fix)
    vec = pl.BlockSpec((1, D_MODEL), fix)
    return pl.pallas_call(
        _merge_kernel, grid=(m // tm,),
        in_specs=[y512, y512, y512, gate(1), gate(2), gate(3), pl.BlockSpec((tm, D_MODEL), row),
                  wsm, wsm, wsm, pl.BlockSpec((D_MODEL, D_MODEL), fix), vec, vec],
        out_specs=pl.BlockSpec((tm, D_MODEL), row),
        out_shape=jax.ShapeDtypeStruct((m, D_MODEL), F32),
        compiler_params=_cp(("parallel",)), name="merge",
    )(ya, yb, yc, hpc, hpc, hpc, x, woa, wob, woc, wmo, lg, lb)


def _xattn_kernel(q_ref, k_ref, v_ref, o_ref):
    scale = X_HEAD_DIM ** -0.5
    for h in range(X_HEADS):
        cs = slice(h * X_HEAD_DIM, (h + 1) * X_HEAD_DIM)
        s = _dot_nt(q_ref[0, :, cs], k_ref[0, :, cs]) * scale
        p = jnp.exp(s - jnp.max(s, axis=-1, keepdims=True))
        p = p / jnp.sum(p, axis=-1, keepdims=True)
        o_ref[0, :, cs] = _dot(p, v_ref[0, :, cs])


def _xattn(q3, k3, v3, tm):
    g, r, _ = q3.shape
    mem = k3.shape[1]
    qs = pl.BlockSpec((1, tm, D_MODEL), lambda i, t: (i, t, 0))
    ms = pl.BlockSpec((1, mem, D_MODEL), lambda i, t: (i, 0, 0))
    return pl.pallas_call(
        _xattn_kernel, grid=(g, r // tm), in_specs=[qs, ms, ms], out_specs=qs,
        out_shape=jax.ShapeDtypeStruct(q3.shape, F32),
        compiler_params=_cp(("parallel", "parallel")), name="xattn")(q3, k3, v3)


def _router_kernel(x_ref, wr_ref, br_ref, tri_ref, idx_ref, gate_ref, rank_ref, cnt_ref, carry_ref):
    @pl.when(pl.program_id(0) == 0)
    def _():
        carry_ref[...] = jnp.zeros(carry_ref.shape, F32)

    logits = _dot_hl(x_ref[...], wr_ref[...]) + br_ref[...]
    tm = logits.shape[0]
    lane = _iota((tm, LANES), 1)
    lane_f = lane.astype(F32)
    lg = logits
    onehots, tops, firsts = [], [], []
    for _ in range(TOP_K):
        m = jnp.max(lg, axis=-1, keepdims=True)
        first = jnp.min(jnp.where(lg == m, lane_f, float(LANES)), axis=-1, keepdims=True)
        pick = lane_f == first
        lg = jnp.where(pick, -jnp.inf, lg)
        onehots.append(jnp.where(pick, 1.0, 0.0))
        tops.append(m)
        firsts.append(first)
    es = [jnp.exp(t - tops[0]) for t in tops]
    den = es[0] + es[1] + es[2] + es[3]
    all_hot = onehots[0] + onehots[1] + onehots[2] + onehots[3]
    before = jnp.dot(tri_ref[...], _bf(all_hot), preferred_element_type=F32) + carry_ref[...]
    idx_o = jnp.zeros((tm, LANES), F32)
    gate_o = jnp.zeros((tm, LANES), F32)
    rank_o = jnp.zeros((tm, LANES), F32)
    for k in range(TOP_K):
        rk = jnp.sum(onehots[k] * before, axis=-1, keepdims=True)
        idx_o = jnp.where(lane == k, firsts[k], idx_o)
        gate_o = jnp.where(lane == k, es[k] / den, gate_o)
        rank_o = jnp.where(lane == k, rk, rank_o)
    idx_ref[...] = idx_o.astype(jnp.int32)
    gate_ref[...] = gate_o
    rank_ref[...] = rank_o.astype(jnp.int32)
    carry_ref[...] = carry_ref[...] + jnp.sum(all_hot, axis=0, keepdims=True)
    cnt_ref[...] = carry_ref[...].astype(jnp.int32)


def _router(x, wr_pad, br_pad, tm=512):
    n = x.shape[0]
    tri = jnp.tril(jnp.ones((tm, tm), F32), -1).astype(BF16)
    row = lambda i: (i, 0)
    fix = lambda i: (0, 0)
    o = pl.BlockSpec((tm, LANES), row)
    return pl.pallas_call(
        _router_kernel, grid=(n // tm,),
        in_specs=[pl.BlockSpec((tm, D_MODEL), row), pl.BlockSpec((D_MODEL, LANES), fix),
                  pl.BlockSpec((1, LANES), fix), pl.BlockSpec((tm, tm), fix)],
        out_specs=[o, o, o, pl.BlockSpec((1, LANES), fix)],
        out_shape=[jax.ShapeDtypeStruct((n, LANES), jnp.int32), jax.ShapeDtypeStruct((n, LANES), F32),
                   jax.ShapeDtypeStruct((n, LANES), jnp.int32),
                   jax.ShapeDtypeStruct((1, LANES), jnp.int32)],
        scratch_shapes=[pltpu.VMEM((1, LANES), F32)],
        compiler_params=_cp(("arbitrary",)), name="router")(x, wr_pad, br_pad, tri)


def _dispatch_kernel(dest_ref, x_ref, xs_in_ref, xs_ref, sem, *, tm):
    del xs_in_ref

    def row_copy(t, k):
        d = dest_ref[t * TOP_K + k]
        return pltpu.make_async_copy(x_ref.at[pl.ds(t, 1), :], xs_ref.at[pl.ds(d, 1), :], sem)

    def issue(t, c):
        for k in range(TOP_K):
            row_copy(t, k).start()
        return c

    def drain(t, c):
        for k in range(TOP_K):
            row_copy(t, k).wait()
        return c

    lax.fori_loop(0, tm, issue, 0)
    lax.fori_loop(0, tm, drain, 0)


def _dispatch(x, dest_flat, xs_zero, tm=256):
    n = x.shape[0]
    return pl.pallas_call(
        functools.partial(_dispatch_kernel, tm=tm), grid=(n // tm,),
        in_specs=[pl.BlockSpec((tm * TOP_K,), lambda i: (i,), memory_space=pltpu.SMEM),
                  pl.BlockSpec((tm, D_MODEL), lambda i: (i, 0)),
                  pl.BlockSpec(memory_space=pl.ANY)],
        out_specs=pl.BlockSpec(memory_space=pl.ANY),
        out_shape=jax.ShapeDtypeStruct(xs_zero.shape, F32),
        scratch_shapes=[pltpu.SemaphoreType.DMA(())],
        input_output_aliases={2: 0},
        compiler_params=_cp(("arbitrary",)), name="moe_dispatch")(dest_flat, x, xs_zero)


def _ffn_kernel(be_ref, nu_ref, x_ref, wgu_ref, bgu_ref, wd_ref, bd_ref, o_ref, wgu_b, wd_b):
    i = pl.program_id(0)
    prev = be_ref[jnp.maximum(i - 1, 0)]
    used = i < nu_ref[0]

    @pl.when(used & ((i == 0) | (be_ref[i] != prev)))
    def _():
        wgu_b[...] = _bf(wgu_ref[0])
        wd_b[...] = _bf(wd_ref[0])

    @pl.when(used)
    def _():
        hgu = jnp.dot(_bf(x_ref[...]), wgu_b[...], preferred_element_type=F32) + bgu_ref[0]
        gate = jnp.minimum(hgu[:, :D_FF], SWIGLU_LIMIT)
        up = jnp.clip(hgu[:, D_FF:], -SWIGLU_LIMIT, SWIGLU_LIMIT)
        act = (up + 1.0) * gate * jax.nn.sigmoid(gate * SWIGLU_ALPHA)
        o_ref[...] = jnp.dot(_bf(act), wd_b[...], preferred_element_type=F32) + bd_ref[0]

    @pl.when(jnp.logical_not(used))
    def _():
        o_ref[...] = jnp.zeros(o_ref.shape, F32)


def _expert_ffn(xs, blk_e, n_used, w_gu, b_gu, w_d, b_d):
    n_blocks = xs.shape[0] // MOE_BLK
    rowi = lambda i, be, nu: (jnp.minimum(i, nu[0] - 1), 0)
    rowo = lambda i, be, nu: (i, 0)
    exp3 = lambda i, be, nu: (be[i], 0, 0)
    grid_spec = pltpu.PrefetchScalarGridSpec(
        num_scalar_prefetch=2, grid=(n_blocks,),
        in_specs=[pl.BlockSpec((MOE_BLK, D_MODEL), rowi),
                  pl.BlockSpec((1, D_MODEL, 2 * D_FF), exp3), pl.BlockSpec((1, 1, 2 * D_FF), exp3),
                  pl.BlockSpec((1, D_FF, D_MODEL), exp3), pl.BlockSpec((1, 1, D_MODEL), exp3)],
        out_specs=pl.BlockSpec((MOE_BLK, D_MODEL), rowo),
        scratch_shapes=[pltpu.VMEM((D_MODEL, 2 * D_FF), BF16), pltpu.VMEM((D_FF, D_MODEL), BF16)])
    return pl.pallas_call(
        _ffn_kernel, grid_spec=grid_spec, out_shape=jax.ShapeDtypeStruct(xs.shape, F32),
        compiler_params=_cp(("arbitrary",)), name="expert_ffn",
    )(blk_e, n_used, xs, w_gu, b_gu.reshape(N_EXPERTS, 1, -1), w_d, b_d.reshape(N_EXPERTS, 1, -1))


def _combine_kernel(dest_ref, gate_ref, x_ref, lg_ref, lb_ref, yb_ref, o_ref, buf, sem, *, tm):
    def row_copy(t, k):
        d = dest_ref[t * TOP_K + k]
        return pltpu.make_async_copy(yb_ref.at[pl.ds(d, 1), :], buf.at[k, pl.ds(t, 1), :], sem)

    def issue(t, c):
        for k in range(TOP_K):
            row_copy(t, k).start()
        return c

    def drain(t, c):
        for k in range(TOP_K):
            row_copy(t, k).wait()
        return c

    lax.fori_loop(0, tm, issue, 0)
    lax.fori_loop(0, tm, drain, 0)
    gates = gate_ref[...]
    y = buf[0] * gates[:, 0:1]
    for k in range(1, TOP_K):
        y = y + buf[k] * gates[:, k:k + 1]
    o_ref[...] = _ln(DN_ALPHA * x_ref[...] + y, lg_ref[...], lb_ref[...])


def _combine(yb, dest_flat, gates, x, lg, lb, tm=256):
    n = x.shape[0]
    row = lambda i: (i, 0)
    fix = lambda i: (0, 0)
    return pl.pallas_call(
        functools.partial(_combine_kernel, tm=tm), grid=(n // tm,),
        in_specs=[pl.BlockSpec((tm * TOP_K,), lambda i: (i,), memory_space=pltpu.SMEM),
                  pl.BlockSpec((tm, LANES), row), pl.BlockSpec((tm, D_MODEL), row),
                  pl.BlockSpec((1, D_MODEL), fix), pl.BlockSpec((1, D_MODEL), fix),
                  pl.BlockSpec(memory_space=pl.ANY)],
        out_specs=pl.BlockSpec((tm, D_MODEL), row),
        out_shape=jax.ShapeDtypeStruct((n, D_MODEL), F32),
        scratch_shapes=[pltpu.VMEM((TOP_K, tm, D_MODEL), F32), pltpu.SemaphoreType.DMA(())],
        compiler_params=_cp(("arbitrary",)), name="moe_combine")(dest_flat, gates, x, lg, lb, yb)


def _moe(x, wr_pad, br_pad, w_gu, b_gu, w_d, b_d, lg, lb):
    n = x.shape[0]
    idx, gates, rank, cnt = _router(x, wr_pad, br_pad)
    counts = cnt[0, :N_EXPERTS]
    padded = (counts + MOE_BLK - 1) // MOE_BLK * MOE_BLK
    pad_end = jnp.cumsum(padded)
    pad_start = pad_end - padded
    e_sel = idx[:, :TOP_K]
    dest = (pad_start[e_sel] + rank[:, :TOP_K]).reshape(-1).astype(jnp.int32)
    n_blocks = -(-(n * TOP_K) // MOE_BLK) + N_EXPERTS
    blk_e = jnp.sum(pad_end[None, :] <= (jnp.arange(n_blocks) * MOE_BLK)[:, None], axis=1)
    blk_e = jnp.minimum(blk_e, N_EXPERTS - 1).astype(jnp.int32)
    n_used = (pad_end[-1:] // MOE_BLK).astype(jnp.int32)
    xs = _dispatch(x, dest, jnp.zeros((n_blocks * MOE_BLK, D_MODEL), F32))
    yb = _expert_ffn(xs, blk_e, n_used, w_gu, b_gu, w_d, b_d)
    return _combine(yb, dest, gates, x, lg, lb)


def _layer_consts(l, w_in, gdn_conv_w, gdn_a_log, gdn_dt_bias, sgu_norm_g, sgu_norm_b, sgu_w, sgu_b, ts):
    wi = w_in[l]
    w_a = _bf(wi[:, :W_A])
    w_s = _bf(jnp.pad(wi[:, OFF_BB:OFF_CU], ((0, 0), (0, LANES - 2 * B_HEADS))))
    w_c = _bf(wi[:, OFF_CU:])
    lane = jnp.arange(512) // B_DK
    head = jnp.arange(LANES)
    bd = (lane[:, None] == lane[None, :]).astype(BF16)
    eb = (head[:, None] == lane[None, :]).astype(BF16)
    eg = (head[:, None] == lane[None, :] + B_HEADS).astype(BF16)
    alog = jnp.pad(gdn_a_log[l], (B_HEADS, LANES - 2 * B_HEADS)).reshape(1, LANES)
    dtb = jnp.pad(gdn_dt_bias[l], (B_HEADS, LANES - 2 * B_HEADS)).reshape(1, LANES)
    consts = (gdn_conv_w[l], bd, eb, eg, alog, dtb)
    sg, sb = sgu_norm_g[l].reshape(1, -1), sgu_norm_b[l].reshape(1, -1)
    tri = jnp.tril(jnp.ones((C_CHUNK, C_CHUNK), F32))
    ws_p = _bf(sgu_w[l] * tri)
    bx_p = jnp.repeat(sgu_b[l].T, C_CHUNK, axis=1)
    wr_s = jnp.repeat(sgu_w[l][:, :ts, :ts].transpose(1, 2, 0).reshape(ts * ts, C_GROUPS), C_CHUNK, axis=1)
    br_s = jnp.repeat(sgu_b[l][:, :ts].T, C_CHUNK, axis=1)
    return w_a, w_s, w_c, consts, (sg, sb, ws_p, bx_p), (sg, sb, wr_s, br_s)


def kernel(x_prompt, x_sample, mem_prompt, cache_moba_k, cache_moba_v, page_table, state_gdn, state_gdn_conv, cache_mem_k, cache_mem_v, w_in, gdn_conv_w, gdn_a_log, gdn_dt_bias, gdn_norm_g, sgu_norm_g, sgu_norm_b, sgu_w, sgu_b, w_out_a, w_out_b, w_out_c, w_mix_out, w_xq, w_xk, w_xv, w_xo, w_router, b_router, w_gate_up, b_gate_up, w_down, b_down, ln_g, ln_b):
    bp, seq, _ = x_prompt.shape
    db, ts, _ = x_sample.shape
    n_pages = page_table.shape[1]
    n_pool, _, page, _, _ = cache_moba_k.shape
    mem_len = mem_prompt.shape[1]
    np_, ns = bp * seq, db * ts
    ck = cache_moba_k.reshape(n_pool, DEPTH, page, A_WIDTH)
    cv = cache_moba_v.reshape(n_pool, DEPTH, page, A_WIDTH)
    pt_flat = page_table.reshape(-1).astype(jnp.int32)
    mem2 = mem_prompt.reshape(bp * mem_len, D_MODEL)
    pad8 = lambda a: jnp.pad(a.reshape(db, ts, -1), ((0, 0), (0, 8 - ts), (0, 0)))

    yp = x_prompt.reshape(np_, D_MODEL)
    ys = x_sample.reshape(ns, D_MODEL)
    outs = {k: [] for k in ("kp", "vp", "ks", "vs", "sp", "ss", "cp", "cs", "mk", "mv", "vr")}
    for l in range(DEPTH):
        w_a, w_s, w_c, consts, sgu_p, sgu_s = _layer_consts(
            l, w_in, gdn_conv_w, gdn_a_log, gdn_dt_bias, sgu_norm_g, sgu_norm_b, sgu_w, sgu_b, ts)
        woa, wob, woc, wmo = _bf(w_out_a[l]), _bf(w_out_b[l]), _bf(w_out_c[l]), _bf(w_mix_out[l])
        wxq, wxo = _bf(w_xq[l]), _bf(w_xo[l])
        lg = [ln_g[l, i].reshape(1, -1) for i in range(3)]
        lb = [ln_b[l, i].reshape(1, -1) for i in range(3)]
        ng = gdn_norm_g[l].reshape(1, -1)
        wr_pad = jnp.pad(w_router[l], ((0, 0), (0, LANES - N_EXPERTS)))
        br_pad = jnp.pad(b_router[l], (0, LANES - N_EXPERTS), constant_values=NEG).reshape(1, LANES)

        mk = _matmul(mem2, _bf(w_xk[l]), 512, 512)
        mv = _matmul(mem2, _bf(w_xv[l]), 512, 512)
        hpa = _matmul(yp, w_a, 1024, 512)
        hps = _matmul(yp, w_s, 1024, LANES)
        hpc = _matmul(yp, w_c, 1024, 512)
        qn, kn, vv, bx, gx, yc = _post_prompt(hpa, hps, hpc, consts, sgu_p, bp, seq)
        ya = _moba_prompt(hpa.reshape(bp, seq, W_A), bp, seq).reshape(np_, A_WIDTH)
        s0 = jnp.zeros((bp, B_HEADS, B_DK, B_DK), F32)
        yb, sp = _gdn(qn, kn, vv, bx, gx, hpa, OFF_BZ // 512, s0, ng, bp, seq, GDN_CHUNK, 1)
        x1 = _merge(ya, yb, yc, hpc, yp, woa, wob, woc, wmo, lg[0], lb[0])
        qx = _matmul(x1, wxq, 1024, 512)
        ox = _xattn(qx.reshape(bp, seq, D_MODEL), mk.reshape(bp, mem_len, D_MODEL),
                    mv.reshape(bp, mem_len, D_MODEL), 512).reshape(np_, D_MODEL)
        x2p = _proj_res_ln(ox, wxo, x1, lg[1], lb[1], 512)
        hpa3 = hpa.reshape(bp, seq, W_A)
        outs["kp"].append(hpa3[:, :, 512:1024].reshape(bp, seq, A_HEADS, A_HEAD_DIM))
        outs["vp"].append(hpa3[:, :, 1024:1536].reshape(bp, seq, A_HEADS, A_HEAD_DIM))
        outs["cp"].append(hpa3[:, seq - (CONV_W - 1):, 1536:3072])
        outs["sp"].append(sp)
        outs["mk"].append(mk.reshape(bp, mem_len, X_HEADS, X_HEAD_DIM))
        outs["mv"].append(mv.reshape(bp, mem_len, X_HEADS, X_HEAD_DIM))

        hpa_s = _matmul(ys, w_a, 512, 512)
        hps_s = _matmul(ys, w_s, 512, LANES)
        hpc_s = _matmul(ys, w_c, 512, 512)
        qn_s, kn_s, v_s, bx_s, gx_s, yc_s, vr_s = _post_sample(
            hpa_s, hps_s, hpc_s, state_gdn_conv[:, l], consts, sgu_s, db, ts)
        hpa_s3 = hpa_s.reshape(db, ts, W_A)
        ya_s = _moba_sample(pad8(hpa_s3[:, :, :512]), pad8(hpa_s3[:, :, 512:1024]),
                            pad8(hpa_s3[:, :, 1024:1536]), ck, cv, pt_flat, l, db, n_pages, ts)
        ya_s = ya_s[:, :ts].reshape(ns, A_WIDTH)
        gdn_in = [pad8(a).reshape(db * 8, 512) for a in (qn_s, kn_s, v_s, bx_s, gx_s)]
        z_s = pad8(hpa_s3[:, :, OFF_BZ:OFF_BB]).reshape(db * 8, 512)
        yb_s, ss = _gdn(*gdn_in, z_s, 0, state_gdn[:, l], ng, db, 8, 8, 2)
        yb_s = yb_s.reshape(db, 8, 512)[:, :ts].reshape(ns, 512)
        x1s = _merge(ya_s, yb_s, yc_s.reshape(ns, 512), hpc_s, ys, woa, wob, woc, wmo, lg[0], lb[0])
        qx_s = _matmul(x1s, wxq, 512, 512)
        ox_s = _xattn(pad8(qx_s), cache_mem_k[:, l].reshape(db, mem_len, D_MODEL),
                      cache_mem_v[:, l].reshape(db, mem_len, D_MODEL), 8)
        ox_s = ox_s[:, :ts].reshape(ns, D_MODEL)
        x2s = _proj_res_ln(ox_s, wxo, x1s, lg[1], lb[1], 512)
        outs["ks"].append(hpa_s3[:, :, 512:1024].reshape(db, ts, A_HEADS, A_HEAD_DIM))
        outs["vs"].append(hpa_s3[:, :, 1024:1536].reshape(db, ts, A_HEADS, A_HEAD_DIM))
        outs["cs"].append(hpa_s3[:, ts - (CONV_W - 1):, 1536:3072])
        outs["ss"].append(ss)
        outs["vr"].append(vr_s.reshape(db, ts, C_WIDTH))

        x3 = _moe(jnp.concatenate([x2p, x2s], axis=0), wr_pad, br_pad, w_gate_up[l], b_gate_up[l],
                  w_down[l], b_down[l], lg[2], lb[2])
        yp, ys = x3[:np_], x3[np_:]

    st = lambda k: jnp.stack(outs[k], axis=1)
    return (yp.reshape(bp, seq, D_MODEL), ys.reshape(db, ts, D_MODEL), st("kp"), st("vp"), st("ks"),
            st("vs"), st("sp"), st("ss"), st("cp"), st("cs"), st("mk"), st("mv"), st("vr"))
```

```python
import functools

import jax
import jax.numpy as jnp
from jax import lax
from jax.experimental import pallas as pl
from jax.experimental.pallas import tpu as pltpu

F32 = jnp.float32
BF16 = jnp.bfloat16

D_MODEL = 1024
DEPTH = 2
A_HEADS = 8
A_HEAD_DIM = 64
A_WIDTH = 512
MOBA_BLOCK = 256
MOBA_TOPK = 3
B_HEADS = 8
B_DK = 64
B_WIDTH = 512
B_CONV_CH = 1536
CONV_W = 4
GDN_CHUNK = 64
C_GROUPS = 4
C_CHUNK = 128
C_WIDTH = 512
X_HEADS = 4
X_HEAD_DIM = 256
N_EXPERTS = 32
TOP_K = 4
D_FF = 1024
SWIGLU_LIMIT = 7.0
SWIGLU_ALPHA = 1.702
DN_ALPHA = (2 * DEPTH) ** 0.25
LN_EPS = 1e-5
RMS_EPS = 1e-6

OFF_BZ = 3072
OFF_BB = 3584
OFF_CU = 3600
P_IN = 7696
W_A = 3584
W_C = P_IN - OFF_CU
LANES = 128
HEAD_PAIRS = B_WIDTH // LANES
MOE_BLK = 256
KV_GROUP = 4
NEG = -1e30
VMEM_LIMIT = 56 * 1024 * 1024


def _cp(sem, vmem=VMEM_LIMIT):
    return pltpu.CompilerParams(dimension_semantics=sem, vmem_limit_bytes=vmem)


def _tile(m, pref):
    if m <= pref:
        return m
    t = pref - pref % 8
    while m % t:
        t -= 8
    return t


def _bf(x):
    return x.astype(BF16)


def _dot(a, b):
    return jnp.dot(_bf(a), _bf(b), preferred_element_type=F32)


_NT = (((1,), (1,)), ((), ()))


def _dot_nt(a, b):
    return lax.dot_general(_bf(a), _bf(b), _NT, preferred_element_type=F32)


def _split(a, n):
    parts, r = [], a
    for _ in range(n):
        p = r.astype(BF16)
        parts.append(p)
        r = r - p.astype(F32)
    return parts


def _dot_x(a, b_bf, n):
    acc = None
    for p in _split(a, n):
        d = jnp.dot(p, b_bf, preferred_element_type=F32)
        acc = d if acc is None else acc + d
    return acc


def _xdot(b_bf, a, n, nt=False):
    acc = None
    for p in _split(a, n):
        if nt:
            d = lax.dot_general(b_bf, p, _NT, preferred_element_type=F32)
        else:
            d = jnp.dot(b_bf, p, preferred_element_type=F32)
        acc = d if acc is None else acc + d
    return acc


def _dot_nt_hl(a, b):
    ah, al = _split(a, 2)
    bh, bl = _split(b, 2)
    f = lambda x, y: lax.dot_general(x, y, _NT, preferred_element_type=F32)
    return f(ah, bh) + f(ah, bl) + f(al, bh)


def _dot_hl(a, b):
    ah, al = _split(a, 2)
    bh, bl = _split(b, 2)
    f = lambda x, y: jnp.dot(x, y, preferred_element_type=F32)
    return f(ah, bh) + f(ah, bl) + f(al, bh)


def _iota(shape, dim):
    return lax.broadcasted_iota(jnp.int32, shape, dim)


def _ln(x, g, b):
    mu = jnp.mean(x, axis=-1, keepdims=True)
    xc = x - mu
    var = jnp.mean(xc * xc, axis=-1, keepdims=True)
    return xc * lax.rsqrt(var + LN_EPS) * g + b


def _softplus(x):
    return jnp.maximum(x, 0.0) + jnp.log(1.0 + jnp.exp(-jnp.abs(x)))


def _top_mask(g, idx, n_idx, k):
    sel = jnp.zeros(g.shape, F32)
    for _ in range(k):
        m = jnp.max(g, axis=-1, keepdims=True)
        first = jnp.min(jnp.where(g == m, idx, float(n_idx)), axis=-1, keepdims=True)
        pick = (idx == first) & (m > -jnp.inf)
        sel = jnp.where(pick, 1.0, sel)
        g = jnp.where(pick, -jnp.inf, g)
    return sel


def _mm_kernel(x_ref, w_ref, o_ref, xb_ref):
    @pl.when(pl.program_id(1) == 0)
    def _():
        xb_ref[...] = _bf(x_ref[...])
    o_ref[...] = jnp.dot(xb_ref[...], w_ref[...], preferred_element_type=F32)


def _matmul(x, w_bf, tm, tn):
    m, k = x.shape
    n = w_bf.shape[1]
    tm, tn = _tile(m, tm), min(tn, n)
    return pl.pallas_call(
        _mm_kernel, grid=(m // tm, n // tn),
        in_specs=[pl.BlockSpec((tm, k), lambda i, j: (i, 0)),
                  pl.BlockSpec((k, tn), lambda i, j: (0, j))],
        out_specs=pl.BlockSpec((tm, tn), lambda i, j: (i, j)),
        out_shape=jax.ShapeDtypeStruct((m, n), F32),
        scratch_shapes=[pltpu.VMEM((tm, k), BF16)],
        compiler_params=_cp(("parallel", "arbitrary")), name="matmul")(x, w_bf)


def _proj_ln_kernel(a_ref, w_ref, r_ref, g_ref, b_ref, o_ref):
    y = jnp.dot(_bf(a_ref[...]), w_ref[...], preferred_element_type=F32)
    o_ref[...] = _ln(DN_ALPHA * r_ref[...] + y, g_ref[...], b_ref[...])


def _proj_res_ln(a, w_bf, res, g, b, tm):
    m, k = a.shape
    tm = _tile(m, tm)
    row = lambda i: (i, 0)
    fix = lambda i: (0, 0)
    return pl.pallas_call(
        _proj_ln_kernel, grid=(m // tm,),
        in_specs=[pl.BlockSpec((tm, k), row), pl.BlockSpec((k, D_MODEL), fix),
                  pl.BlockSpec((tm, D_MODEL), row), pl.BlockSpec((1, D_MODEL), fix),
                  pl.BlockSpec((1, D_MODEL), fix)],
        out_specs=pl.BlockSpec((tm, D_MODEL), row),
        out_shape=jax.ShapeDtypeStruct((m, D_MODEL), F32),
        compiler_params=_cp(("parallel",)), name="proj_res_ln")(a, w_bf, res, g, b)


def _gdn_pre(c, hs, bd, eb, eg, alog, dtb):
    q = c[:, :512]
    k = c[:, 512:1024]
    v = c[:, 1024:]
    qn = q * lax.rsqrt(_dot_x(q * q, bd, 2) + RMS_EPS) * (B_DK ** -0.5)
    kn = k * lax.rsqrt(_dot_x(k * k, bd, 2) + RMS_EPS)
    beta = jax.nn.sigmoid(hs)
    g = -jnp.exp(alog) * _softplus(hs + dtb)
    return qn, kn, v, _dot_x(beta, eb, 3), _dot_x(g, eg, 3)


def _post_p_kernel(bqkv_ref, hs_ref, cuv_ref, cw_ref, bd_ref, eb_ref, eg_ref, alog_ref, dtb_ref,
                   sg_ref, sb_ref, ws_ref, bx_ref,
                   qn_ref, kn_ref, v_ref, beta_ref, g_ref, yc_ref, xe_ref, *, tm):
    @pl.when(pl.program_id(1) == 0)
    def _():
        xe_ref[0:8, :] = jnp.zeros((8, B_CONV_CH), F32)
    xin = bqkv_ref[...]
    xe_ref[8:8 + tm, :] = xin
    cw = cw_ref[...]
    acc = xe_ref[5:5 + tm, :] * cw[0:1]
    acc = acc + xe_ref[6:6 + tm, :] * cw[1:2]
    acc = acc + xe_ref[7:7 + tm, :] * cw[2:3]
    acc = acc + xin * cw[3:4]
    xe_ref[0:8, :] = xe_ref[tm:tm + 8, :]
    c = acc * jax.nn.sigmoid(acc)
    qn, kn, v, bx, gx = _gdn_pre(c, hs_ref[...], bd_ref[...], eb_ref[...], eg_ref[...],
                                 alog_ref[...], dtb_ref[...])
    qn_ref[...] = qn
    kn_ref[...] = kn
    v_ref[...] = v
    beta_ref[...] = bx
    g_ref[...] = gx
    u = jax.nn.gelu(cuv_ref[:, :C_WIDTH])
    vr = _bf(_ln(jax.nn.gelu(cuv_ref[:, C_WIDTH:]), sg_ref[...], sb_ref[...]))
    for ch in range(tm // C_CHUNK):
        rs = slice(ch * C_CHUNK, (ch + 1) * C_CHUNK)
        for gi in range(C_GROUPS):
            cs = slice(gi * LANES, (gi + 1) * LANES)
            mixed = jnp.dot(ws_ref[gi], vr[rs, cs], preferred_element_type=F32) + bx_ref[:, cs]
            yc_ref[rs, cs] = u[rs, cs] * mixed


def _post_prompt(hpa, hps, hpc, consts, sgu, bsz, seq, tm=256):
    n = bsz * seq
    nt = seq // tm
    row = lambda b, t: (b * nt + t, 0)
    fix2 = lambda b, t: (0, 0)
    fix3 = lambda b, t: (0, 0, 0)
    cw, bd, eb, eg, alog, dtb = consts
    sg, sb, ws, bx = sgu
    o512 = jax.ShapeDtypeStruct((n, 512), F32)
    return pl.pallas_call(
        functools.partial(_post_p_kernel, tm=tm), grid=(bsz, nt),
        in_specs=[pl.BlockSpec((tm, B_CONV_CH), lambda b, t: (b * nt + t, 1)),
                  pl.BlockSpec((tm, LANES), row),
                  pl.BlockSpec((tm, 2 * C_WIDTH), row),
                  pl.BlockSpec((CONV_W, B_CONV_CH), fix2),
                  pl.BlockSpec((512, 512), fix2), pl.BlockSpec((LANES, 512), fix2),
                  pl.BlockSpec((LANES, 512), fix2), pl.BlockSpec((1, LANES), fix2),
                  pl.BlockSpec((1, LANES), fix2),
                  pl.BlockSpec((1, C_WIDTH), fix2), pl.BlockSpec((1, C_WIDTH), fix2),
                  pl.BlockSpec((C_GROUPS, C_CHUNK, C_CHUNK), fix3),
                  pl.BlockSpec((C_CHUNK, C_WIDTH), fix2)],
        out_specs=[pl.BlockSpec((tm, 512), row)] * 6,
        out_shape=[o512] * 6,
        scratch_shapes=[pltpu.VMEM((tm + 8, B_CONV_CH), F32)],
        compiler_params=_cp(("parallel", "arbitrary")), name="post_prompt",
    )(hpa, hps, hpc, cw, bd, eb, eg, alog, dtb, sg, sb, ws, bx)


def _post_s_kernel(hpa_ref, hps_ref, hpc_ref, prev_ref, cw_ref, bd_ref, eb_ref, eg_ref, alog_ref,
                   dtb_ref, sg_ref, sb_ref, wr_ref, br_ref,
                   qn_ref, kn_ref, v_ref, beta_ref, g_ref, yc_ref, vr_ref, *, ts):
    cw = cw_ref[...]
    xe = [prev_ref[:, j * B_CONV_CH:(j + 1) * B_CONV_CH] for j in range(CONV_W - 1)]
    xe += [hpa_ref[:, p * W_A + 1536:p * W_A + 3072] for p in range(ts)]
    vrows = []
    for p in range(ts):
        acc = xe[p] * cw[0:1]
        for j in range(1, CONV_W):
            acc = acc + xe[p + j] * cw[j:j + 1]
        c = acc * jax.nn.sigmoid(acc)
        hs = hps_ref[:, p * LANES:(p + 1) * LANES]
        qn, kn, v, bx, gx = _gdn_pre(c, hs, bd_ref[...], eb_ref[...], eg_ref[...],
                                     alog_ref[...], dtb_ref[...])
        ps = slice(p * 512, (p + 1) * 512)
        qn_ref[:, ps] = qn
        kn_ref[:, ps] = kn
        v_ref[:, ps] = v
        beta_ref[:, ps] = bx
        g_ref[:, ps] = gx
        vr = _ln(jax.nn.gelu(hpc_ref[:, p * W_C + C_WIDTH:p * W_C + 2 * C_WIDTH]),
                 sg_ref[...], sb_ref[...])
        vr_ref[:, ps] = vr
        vrows.append(vr)
        mixed = br_ref[p:p + 1, :]
        for j in range(p + 1):
            mixed = mixed + wr_ref[p * ts + j:p * ts + j + 1, :] * vrows[j]
        yc_ref[:, ps] = jax.nn.gelu(hpc_ref[:, p * W_C:p * W_C + C_WIDTH]) * mixed


def _post_sample(hpa, hps, hpc, prev, consts, sgu, db, ts):
    cw, bd, eb, eg, alog, dtb = consts
    sg, sb, wr, br = sgu
    args = (hpa.reshape(db, ts * W_A), hps.reshape(db, ts * LANES), hpc.reshape(db, ts * W_C),
            prev.reshape(db, (CONV_W - 1) * B_CONV_CH), cw, bd, eb, eg, alog, dtb, sg, sb, wr, br)
    o = jax.ShapeDtypeStruct((db, ts * 512), F32)
    return pl.pallas_call(
        functools.partial(_post_s_kernel, ts=ts),
        out_shape=[o] * 7, compiler_params=_cp(None), name="post_sample")(*args)


def _gdn_kernel(q_ref, k_ref, v_ref, bx_ref, gx_ref, s0_ref, o_ref, sf_ref, s_scr,
                *, c_len, bb, npc, n_dbl):
    ci = pl.program_id(1)

    @pl.when(ci == 0)
    def _():
        s_scr[...] = s0_ref[...]

    c2 = 2 * c_len
    ri = _iota((c2, c2), 0)
    cj = _iota((c2, c2), 1)
    same = (ri >= c_len) == (cj >= c_len)
    low = same & (ri >= cj)
    strict = same & (ri > cj)
    eye2 = jnp.where(ri == cj, 1.0, 0.0)
    tril_c = jnp.where(_iota((c_len, c_len), 0) >= _iota((c_len, c_len), 1), 1.0, 0.0).astype(BF16)
    eye_l = jnp.where(_iota((LANES, LANES), 0) == _iota((LANES, LANES), 1), 1.0, 0.0).astype(BF16)
    hsel = jnp.where(_iota((B_HEADS, B_WIDTH), 1) == _iota((B_HEADS, B_WIDTH), 0) * B_DK,
                     1.0, 0.0).astype(BF16)
    place = (_iota((c2, LANES), 1) < B_DK) == (_iota((c2, LANES), 0) < c_len)
    top_rows = _iota((c2, 1), 0) < c_len
    left_cols = _iota((1, c2), 1) < c_len

    prep = {}
    for i in range(bb):
        for n in range(npc):
            rs = slice(n * c_len, (n + 1) * c_len)
            q, k, v = q_ref[i, rs, :], k_ref[i, rs, :], v_ref[i, rs, :]
            bx, gx = bx_ref[i, rs, :], gx_ref[i, rs, :]
            gcb = _xdot(tril_c, gx, 3)
            g2 = jnp.concatenate([gcb, gcb], axis=0)
            gct = _xdot(hsel, g2, 3, nt=True)
            eg = jnp.exp(gcb)
            g_last = gcb[c_len - 1:c_len, :]
            gl = jnp.exp(g_last)
            kb = k * bx
            wide = (k, kb, kb * eg, v * bx, q, q * eg, k * jnp.exp(g_last - gcb))
            for p in range(HEAD_PAIRS):
                ls = slice(p * LANES, (p + 1) * LANES)
                ks, kbs, kbes, vbs, qs, qes, kds = [
                    jnp.where(place, jnp.concatenate([x[:, ls], x[:, ls]], axis=0), 0.0) for x in wide]
                gcol = jnp.where(top_rows, g2[:, p * LANES:p * LANES + 1],
                                 g2[:, p * LANES + B_DK:p * LANES + B_DK + 1])
                grow = jnp.where(left_cols, gct[2 * p:2 * p + 1, :], gct[2 * p + 1:2 * p + 2, :])
                decay = jnp.where(low, jnp.exp(jnp.where(low, gcol - grow, 0.0)), 0.0)
                a_mat = jnp.where(strict, _dot_nt(kbs, ks) * decay, 0.0)
                t_mat = eye2 - a_mat
                pw = a_mat
                for _ in range(n_dbl):
                    pw = _dot(pw, pw)
                    t_mat = t_mat + _dot(t_mat, pw)
                u = _dot(t_mat, vbs)
                w = _dot(t_mat, kbes)
                qk = _dot_nt(qs, ks) * decay
                kd_t = lax.dot_general(eye_l, _bf(kds), _NT, preferred_element_type=F32)
                prep[(i, n, p)] = (u, _bf(w), _bf(qk), _bf(qes), _bf(kd_t), gl[:, ls])

    state = {(i, p): s_scr[i, p] for i in range(bb) for p in range(HEAD_PAIRS)}
    outs = {}
    for n in range(npc):
        for i in range(bb):
            for p in range(HEAD_PAIRS):
                u, w, qk, qes, kd_t, gl = prep[(i, n, p)]
                s = state[(i, p)]
                sb = _bf(s)
                v_new = u - jnp.dot(w, sb, preferred_element_type=F32)
                vb = _bf(v_new)
                o = (jnp.dot(qes, sb, preferred_element_type=F32)
                     + jnp.dot(qk, vb, preferred_element_type=F32))
                outs[(i, n, p)] = o[:c_len] + o[c_len:]
                state[(i, p)] = s * gl + jnp.dot(kd_t, vb, preferred_element_type=F32)
    for (i, n, p), o in outs.items():
        o_ref[i, n * c_len:(n + 1) * c_len, p * LANES:(p + 1) * LANES] = o
    for (i, p), s in state.items():
        s_scr[i, p] = s

    @pl.when(ci == pl.num_programs(1) - 1)
    def _():
        sf_ref[...] = s_scr[...]


def _gdn(qn, kn, v, bx, gx, s0_bd, c_len, bb, npc):
    bsz, t, _ = qn.shape
    n_dbl = max(c_len.bit_length() - 2, 0)
    tile = npc * c_len
    row = pl.BlockSpec((bb, tile, 512), lambda b, c: (b, c, 0))
    st = pl.BlockSpec((bb, HEAD_PAIRS, LANES, LANES), lambda b, c: (b, 0, 0, 0))
    return pl.pallas_call(
        functools.partial(_gdn_kernel, c_len=c_len, bb=bb, npc=npc, n_dbl=n_dbl),
        grid=(bsz // bb, t // tile),
        in_specs=[row] * 5 + [st], out_specs=[row, st],
        out_shape=[jax.ShapeDtypeStruct((bsz, t, 512), F32),
                   jax.ShapeDtypeStruct((bsz, HEAD_PAIRS, LANES, LANES), F32)],
        scratch_shapes=[pltpu.VMEM((bb, HEAD_PAIRS, LANES, LANES), F32)],
        compiler_params=_cp(("parallel", "arbitrary")), name="gdn",
    )(qn, kn, v, bx, gx, s0_bd)


def _state_to_pairs(s):
    b = s.shape[0]
    s5 = s.reshape(b, HEAD_PAIRS, 2, B_DK, B_DK)
    z = jnp.zeros_like(s5[:, :, 0])
    top = jnp.concatenate([s5[:, :, 0], z], axis=-1)
    bot = jnp.concatenate([z, s5[:, :, 1]], axis=-1)
    return jnp.concatenate([top, bot], axis=-2)


def _state_from_pairs(sp):
    b = sp.shape[0]
    return jnp.stack([sp[:, :, :B_DK, :B_DK], sp[:, :, B_DK:, B_DK:]], axis=2).reshape(
        b, B_HEADS, B_DK, B_DK)


def _moba_p_kernel(q_ref, k_ref, v_ref, o_ref, km_ref, *, nb, nbp, n_sel, grp):
    own = pl.program_id(2)
    tq = MOBA_BLOCK

    @pl.when(own == 0)
    def _():
        km_ref[...] = jnp.zeros((nbp, LANES), F32)
        for n in range(nb):
            km_ref[n:n + 1, :] = jnp.mean(k_ref[0, n * MOBA_BLOCK:(n + 1) * MOBA_BLOCK, :],
                                          axis=0, keepdims=True)

    q2 = q_ref[0]
    first_head = _iota((tq, LANES), 1) < A_HEAD_DIM
    qh = [jnp.where(first_head, q2, 0.0), jnp.where(first_head, 0.0, q2)]
    qhb = [_bf(x) for x in qh]
    km = km_ref[...]
    blk = _iota((tq, nbp), 1)
    blk_f = blk.astype(F32)
    sels = []
    for h in range(2):
        g = _dot_nt_hl(qh[h], km)
        g = jnp.where(blk < own, g, -jnp.inf)
        sels.append(_top_mask(g, blk_f, nbp, n_sel))

    start = pl.multiple_of(own * MOBA_BLOCK, MOBA_BLOCK)
    k_own = _bf(k_ref[0, pl.ds(start, MOBA_BLOCK), :])
    v_own = _bf(v_ref[0, pl.ds(start, MOBA_BLOCK), :])
    causal = _iota((tq, MOBA_BLOCK), 1) <= _iota((tq, MOBA_BLOCK), 0)
    scale = A_HEAD_DIM ** -0.5
    init = []
    for h in range(2):
        s = lax.dot_general(qhb[h], k_own, _NT, preferred_element_type=F32)
        s = jnp.where(causal, s * scale, NEG)
        m = jnp.max(s, axis=-1, keepdims=True)
        p = jnp.exp(s - m)
        init += [m, jnp.sum(p, axis=-1, keepdims=True),
                 jnp.dot(_bf(p), v_own, preferred_element_type=F32)]

    span = grp * MOBA_BLOCK

    def body(gi, carry):
        st = pl.multiple_of(gi * span, span)
        kg = _bf(k_ref[0, pl.ds(st, span), :])
        vg = _bf(v_ref[0, pl.ds(st, span), :])
        out = []
        for h in range(2):
            m, l, acc = carry[3 * h:3 * h + 3]
            s = lax.dot_general(qhb[h], kg, _NT, preferred_element_type=F32) * scale
            parts = []
            for c in range(grp):
                selj = jnp.sum(jnp.where(blk == gi * grp + c, sels[h], 0.0), axis=-1, keepdims=True)
                parts.append(jnp.where(selj > 0.0, s[:, c * MOBA_BLOCK:(c + 1) * MOBA_BLOCK], NEG))
            s = jnp.concatenate(parts, axis=1)
            m_new = jnp.maximum(m, jnp.max(s, axis=-1, keepdims=True))
            alpha = jnp.exp(m - m_new)
            p = jnp.exp(s - m_new)
            l = l * alpha + jnp.sum(p, axis=-1, keepdims=True)
            acc = acc * alpha + jnp.dot(_bf(p), vg, preferred_element_type=F32)
            out += [m_new, l, acc]
        return tuple(out)

    n_grp = (own + grp - 1) // grp
    m0, l0, a0, m1, l1, a1 = lax.fori_loop(0, n_grp, body, tuple(init))
    o_ref[0] = jnp.where(first_head, a0 / l0, a1 / l1)


def _moba_prompt(hpa3, bsz, seq):
    nb = seq // MOBA_BLOCK
    nbp = max(8, -(-nb // 8) * 8)
    grp = KV_GROUP if nb % KV_GROUP == 0 else 1
    npair = A_WIDTH // LANES
    return pl.pallas_call(
        functools.partial(_moba_p_kernel, nb=nb, nbp=nbp, n_sel=min(MOBA_TOPK, nb - 1), grp=grp),
        grid=(bsz, npair, nb),
        in_specs=[pl.BlockSpec((1, MOBA_BLOCK, LANES), lambda b, p, i: (b, i, p)),
                  pl.BlockSpec((1, seq, LANES), lambda b, p, i: (b, 0, npair + p)),
                  pl.BlockSpec((1, seq, LANES), lambda b, p, i: (b, 0, 2 * npair + p))],
        out_specs=pl.BlockSpec((1, MOBA_BLOCK, LANES), lambda b, p, i: (b, i, p)),
        out_shape=jax.ShapeDtypeStruct((bsz, seq, A_WIDTH), F32),
        scratch_shapes=[pltpu.VMEM((nbp, LANES), F32)],
        compiler_params=_cp(("parallel", "parallel", "arbitrary")), name="moba_prompt",
    )(hpa3, hpa3, hpa3)


def _moba_s_kernel(pt_ref, q_ref, kn_ref, vn_ref, *refs, n_pages, ts, page):
    kt_pages = refs[:n_pages]
    vt_pages = refs[n_pages:2 * n_pages]
    o_ref, s_ref = refs[2 * n_pages:]
    ppb = MOBA_BLOCK // page
    nb = n_pages // ppb
    rows = A_HEADS * 8
    q8 = q_ref[0]
    lane_head = _iota((8, A_WIDTH), 1) // A_HEAD_DIM
    qbd = jnp.concatenate([jnp.where(lane_head == h, q8, 0.0) for h in range(A_HEADS)], axis=0)
    qbd_b = _bf(qbd)
    scale = A_HEAD_DIM ** -0.5
    col = _iota((A_WIDTH, LANES), 1)
    km = jnp.zeros((A_WIDTH, LANES), F32)
    for n in range(nb):
        tot = None
        for pp in range(ppb):
            sm = jnp.sum(kt_pages[n * ppb + pp][0, 0], axis=1, keepdims=True)
            tot = sm if tot is None else tot + sm
        km = jnp.where(col == n, tot * (1.0 / MOBA_BLOCK), km)
    blk = _iota((rows, LANES), 1)
    g = _dot_hl(qbd, km)
    g = jnp.where(blk < nb, g, -jnp.inf)
    sel = _top_mask(g, blk.astype(F32), LANES, min(MOBA_TOPK, nb))
    s_own = lax.dot_general(qbd_b, _bf(kn_ref[0]), _NT, preferred_element_type=F32) * scale
    kr = _iota((rows, 8), 1)
    qr = _iota((rows, 8), 0) % 8
    s_own = jnp.where((kr <= qr) & (kr < ts), s_own, NEG)
    m = jnp.max(s_own, axis=-1, keepdims=True)
    for pg in range(n_pages):
        s = jnp.dot(qbd_b, _bf(kt_pages[pg][0, 0]), preferred_element_type=F32) * scale
        s = jnp.where(sel[:, pg // ppb:pg // ppb + 1] > 0.0, s, NEG)
        s_ref[:, pg * page:(pg + 1) * page] = s
        m = jnp.maximum(m, jnp.max(s, axis=-1, keepdims=True))
    p_own = jnp.exp(s_own - m)
    l = jnp.sum(p_own, axis=-1, keepdims=True)
    for pg in range(n_pages):
        p = jnp.exp(s_ref[:, pg * page:(pg + 1) * page] - m)
        s_ref[:, pg * page:(pg + 1) * page] = p
        l = l + jnp.sum(p, axis=-1, keepdims=True)
    inv = 1.0 / l
    o = jnp.dot(_bf(p_own * inv), _bf(vn_ref[0]), preferred_element_type=F32)
    for pg in range(n_pages):
        p = s_ref[:, pg * page:(pg + 1) * page] * inv
        o = o + lax.dot_general(_bf(p), _bf(vt_pages[pg][0, 0]), _NT, preferred_element_type=F32)
    out = jnp.zeros((8, A_WIDTH), F32)
    for h in range(A_HEADS):
        out = out + jnp.where(lane_head == h, o[h * 8:(h + 1) * 8, :], 0.0)
    o_ref[0] = out


def _moba_sample(q8, kn8, vn8, cache_kt, cache_vt, pt_flat, layer, db, n_pages, ts):
    page = cache_kt.shape[3]
    tok = pl.BlockSpec((1, 8, A_WIDTH), lambda b, pt: (b, 0, 0))

    def page_spec(p):
        return pl.BlockSpec((1, 1, A_WIDTH, page), lambda b, pt: (pt[b * n_pages + p], layer, 0, 0))

    specs = [page_spec(p) for p in range(n_pages)]
    grid_spec = pltpu.PrefetchScalarGridSpec(
        num_scalar_prefetch=1, grid=(db,),
        in_specs=[tok, tok, tok] + specs + specs,
        out_specs=tok,
        scratch_shapes=[pltpu.VMEM((A_HEADS * 8, n_pages * page), F32)])
    return pl.pallas_call(
        functools.partial(_moba_s_kernel, n_pages=n_pages, ts=ts, page=page),
        grid_spec=grid_spec, out_shape=jax.ShapeDtypeStruct((db, 8, A_WIDTH), F32),
        compiler_params=_cp(("parallel",)), name="moba_sample",
    )(pt_flat, q8, kn8, vn8, *([cache_kt] * n_pages), *([cache_vt] * n_pages))


def _merge_kernel(ya_ref, yb_ref, z_ref, yc_ref, g0_ref, g1_ref, g2_ref, x_ref, bd_ref, ng_ref,
                  woa_ref, wob_ref, woc_ref, wmo_ref, lg_ref, lb_ref, o_ref):
    f = lambda y, w: jnp.dot(_bf(y), w[...], preferred_element_type=F32)
    yb = yb_ref[...]
    z = z_ref[...]
    ms = _dot_x(yb * yb, bd_ref[...], 2) * (1.0 / B_DK)
    yb = yb * lax.rsqrt(ms + RMS_EPS) * ng_ref[...] * (z * jax.nn.sigmoid(z))
    merged = jax.nn.sigmoid(g0_ref[...]) * f(ya_ref[...], woa_ref)
    merged = merged + jax.nn.sigmoid(g1_ref[...]) * f(yb, wob_ref)
    merged = merged + jax.nn.sigmoid(g2_ref[...]) * f(yc_ref[...], woc_ref)
    y = jnp.dot(_bf(merged), wmo_ref[...], preferred_element_type=F32)
    o_ref[...] = _ln(DN_ALPHA * x_ref[...] + y, lg_ref[...], lb_ref[...])


def _merge(ya, yb, hpa, yc, hpc, x, bd, ng, woa, wob, woc, wmo, lg, lb, tm=256):
    m = x.shape[0]
    tm = _tile(m, tm)
    row = lambda i: (i, 0)
    fix = lambda i: (0, 0)
    y512 = pl.BlockSpec((tm, 512), row)
    gate = lambda j: pl.BlockSpec((tm, D_MODEL), lambda i: (i, j))
    wsm = pl.BlockSpec((512, D_MODEL), fix)
    vec = pl.BlockSpec((1, D_MODEL), fix)
    return pl.pallas_call(
        _merge_kernel, grid=(m // tm,),
        in_specs=[y512, y512, pl.BlockSpec((tm, 512), lambda i: (i, OFF_BZ // 512)), y512,
                  gate(1), gate(2), gate(3), pl.BlockSpec((tm, D_MODEL), row),
                  pl.BlockSpec((512, 512), fix), pl.BlockSpec((1, 512), fix),
                  wsm, wsm, wsm, pl.BlockSpec((D_MODEL, D_MODEL), fix), vec, vec],
        out_specs=pl.BlockSpec((tm, D_MODEL), row),
        out_shape=jax.ShapeDtypeStruct((m, D_MODEL), F32),
        compiler_params=_cp(("parallel",)), name="merge",
    )(ya, yb, hpa, yc, hpc, hpc, hpc, x, bd, ng, woa, wob, woc, wmo, lg, lb)


def _attend(q, k, v):
    s = _dot_nt(q, k) * (X_HEAD_DIM ** -0.5)
    p = jnp.exp(s - jnp.max(s, axis=-1, keepdims=True))
    p = p / jnp.sum(p, axis=-1, keepdims=True)
    return _dot(p, v)


def _xattn_kernel(q_ref, k_ref, v_ref, o_ref):
    for h in range(X_HEADS):
        cs = slice(h * X_HEAD_DIM, (h + 1) * X_HEAD_DIM)
        o_ref[0, :, cs] = _attend(q_ref[0, :, cs], k_ref[0, :, cs], v_ref[0, :, cs])


def _xattn(q3, k3, v3, tm):
    g, r, _ = q3.shape
    tm = _tile(r, tm)
    mem = k3.shape[1]
    qs = pl.BlockSpec((1, tm, D_MODEL), lambda i, t: (i, t, 0))
    ms = pl.BlockSpec((1, mem, D_MODEL), lambda i, t: (i, 0, 0))
    return pl.pallas_call(
        _xattn_kernel, grid=(g, r // tm), in_specs=[qs, ms, ms], out_specs=qs,
        out_shape=jax.ShapeDtypeStruct(q3.shape, F32),
        compiler_params=_cp(("parallel", "parallel")), name="xattn")(q3, k3, v3)


def _xattn_cache_kernel(q_ref, k_ref, v_ref, o_ref):
    for h in range(X_HEADS):
        cs = slice(h * X_HEAD_DIM, (h + 1) * X_HEAD_DIM)
        o_ref[0, :, cs] = _attend(q_ref[0, :, cs], k_ref[0, 0, :, h, :], v_ref[0, 0, :, h, :])


def _xattn_cache(q3, cache_k, cache_v, layer):
    g = q3.shape[0]
    mem = cache_k.shape[2]
    qs = pl.BlockSpec((1, 8, D_MODEL), lambda i: (i, 0, 0))
    ms = pl.BlockSpec((1, 1, mem, X_HEADS, X_HEAD_DIM), lambda i: (i, layer, 0, 0, 0))
    return pl.pallas_call(
        _xattn_cache_kernel, grid=(g,), in_specs=[qs, ms, ms], out_specs=qs,
        out_shape=jax.ShapeDtypeStruct(q3.shape, F32),
        compiler_params=_cp(("parallel",)), name="xattn_cache")(q3, cache_k, cache_v)


def _router_kernel(x_ref, wr_ref, br_ref, tri_ref, idx_ref, gate_ref, rank_ref, cnt_ref, carry_ref):
    @pl.when(pl.program_id(0) == 0)
    def _():
        carry_ref[...] = jnp.zeros(carry_ref.shape, F32)

    logits = _dot_hl(x_ref[...], wr_ref[...]) + br_ref[...]
    tm = logits.shape[0]
    lane = _iota((tm, LANES), 1)
    lane_f = lane.astype(F32)
    lg = logits
    onehots, tops, firsts = [], [], []
    for _ in range(TOP_K):
        m = jnp.max(lg, axis=-1, keepdims=True)
        first = jnp.min(jnp.where(lg == m, lane_f, float(LANES)), axis=-1, keepdims=True)
        pick = lane_f == first
        lg = jnp.where(pick, -jnp.inf, lg)
        onehots.append(jnp.where(pick, 1.0, 0.0))
        tops.append(m)
        firsts.append(first)
    es = [jnp.exp(t - tops[0]) for t in tops]
    den = es[0] + es[1] + es[2] + es[3]
    all_hot = onehots[0] + onehots[1] + onehots[2] + onehots[3]
    before = jnp.dot(tri_ref[...], _bf(all_hot), preferred_element_type=F32) + carry_ref[...]
    idx_o = jnp.zeros((tm, LANES), F32)
    gate_o = jnp.zeros((tm, LANES), F32)
    rank_o = jnp.zeros((tm, LANES), F32)
    for k in range(TOP_K):
        rk = jnp.sum(onehots[k] * before, axis=-1, keepdims=True)
        idx_o = jnp.where(lane == k, firsts[k], idx_o)
        gate_o = jnp.where(lane == k, es[k] / den, gate_o)
        rank_o = jnp.where(lane == k, rk, rank_o)
    idx_ref[...] = idx_o.astype(jnp.int32)
    gate_ref[...] = gate_o
    rank_ref[...] = rank_o.astype(jnp.int32)
    carry_ref[...] = carry_ref[...] + jnp.sum(all_hot, axis=0, keepdims=True)
    cnt_ref[...] = carry_ref[...].astype(jnp.int32)


def _router(x, wr_pad, br_pad, tm=512):
    n = x.shape[0]
    tm = _tile(n, tm)
    tri = jnp.tril(jnp.ones((tm, tm), F32), -1).astype(BF16)
    row = lambda i: (i, 0)
    fix = lambda i: (0, 0)
    o = pl.BlockSpec((tm, LANES), row)
    return pl.pallas_call(
        _router_kernel, grid=(n // tm,),
        in_specs=[pl.BlockSpec((tm, D_MODEL), row), pl.BlockSpec((D_MODEL, LANES), fix),
                  pl.BlockSpec((1, LANES), fix), pl.BlockSpec((tm, tm), fix)],
        out_specs=[o, o, o, pl.BlockSpec((1, LANES), fix)],
        out_shape=[jax.ShapeDtypeStruct((n, LANES), jnp.int32), jax.ShapeDtypeStruct((n, LANES), F32),
                   jax.ShapeDtypeStruct((n, LANES), jnp.int32),
                   jax.ShapeDtypeStruct((1, LANES), jnp.int32)],
        scratch_shapes=[pltpu.VMEM((1, LANES), F32)],
        compiler_params=_cp(("arbitrary",)), name="router")(x, wr_pad, br_pad, tri)


def _dispatch_kernel(dest_ref, x_ref, xs_in_ref, xs_ref, sem, *, tm):
    del xs_in_ref

    def row_copy(t, k):
        d = dest_ref[t * TOP_K + k]
        return pltpu.make_async_copy(x_ref.at[pl.ds(t, 1), :], xs_ref.at[pl.ds(d, 1), :], sem)

    def issue(t, c):
        for k in range(TOP_K):
            row_copy(t, k).start(priority=k % 2)
        return c

    def drain(t, c):
        for k in range(TOP_K):
            row_copy(t, k).wait()
        return c

    lax.fori_loop(0, tm, issue, 0)
    lax.fori_loop(0, tm, drain, 0)


def _dispatch(x, dest_flat, xs_zero, tm=256):
    n = x.shape[0]
    return pl.pallas_call(
        functools.partial(_dispatch_kernel, tm=tm), grid=(n // tm,),
        in_specs=[pl.BlockSpec((tm * TOP_K,), lambda i: (i,), memory_space=pltpu.SMEM),
                  pl.BlockSpec((tm, D_MODEL), lambda i: (i, 0)),
                  pl.BlockSpec(memory_space=pl.ANY)],
        out_specs=pl.BlockSpec(memory_space=pl.ANY),
        out_shape=jax.ShapeDtypeStruct(xs_zero.shape, F32),
        scratch_shapes=[pltpu.SemaphoreType.DMA(())],
        input_output_aliases={2: 0},
        compiler_params=_cp(("arbitrary",)), name="moe_dispatch")(dest_flat, x, xs_zero)


def _ffn_kernel(be_ref, nu_ref, x_ref, wgu_ref, bgu_ref, wd_ref, bd_ref, o_ref, wgu_b, wd_b):
    i = pl.program_id(0)
    prev = be_ref[jnp.maximum(i - 1, 0)]
    used = i < nu_ref[0]

    @pl.when(used & ((i == 0) | (be_ref[i] != prev)))
    def _():
        wgu_b[...] = _bf(wgu_ref[0, 0])
        wd_b[...] = _bf(wd_ref[0, 0])

    @pl.when(used)
    def _():
        hgu = jnp.dot(_bf(x_ref[...]), wgu_b[...], preferred_element_type=F32) + bgu_ref[0, 0]
        gate = jnp.minimum(hgu[:, :D_FF], SWIGLU_LIMIT)
        up = jnp.clip(hgu[:, D_FF:], -SWIGLU_LIMIT, SWIGLU_LIMIT)
        act = (up + 1.0) * gate * jax.nn.sigmoid(gate * SWIGLU_ALPHA)
        o_ref[...] = jnp.dot(_bf(act), wd_b[...], preferred_element_type=F32) + bd_ref[0, 0]

    @pl.when(jnp.logical_not(used))
    def _():
        o_ref[...] = jnp.zeros(o_ref.shape, F32)


def _expert_ffn(xs, blk_e, n_used, w_gu, b_gu, w_d, b_d, layer):
    n_blocks = xs.shape[0] // MOE_BLK
    rowi = lambda i, be, nu: (jnp.minimum(i, nu[0] - 1), 0)
    rowo = lambda i, be, nu: (i, 0)
    exp4 = lambda i, be, nu: (layer, be[i], 0, 0)
    grid_spec = pltpu.PrefetchScalarGridSpec(
        num_scalar_prefetch=2, grid=(n_blocks,),
        in_specs=[pl.BlockSpec((MOE_BLK, D_MODEL), rowi),
                  pl.BlockSpec((1, 1, D_MODEL, 2 * D_FF), exp4), pl.BlockSpec((1, 1, 1, 2 * D_FF), exp4),
                  pl.BlockSpec((1, 1, D_FF, D_MODEL), exp4), pl.BlockSpec((1, 1, 1, D_MODEL), exp4)],
        out_specs=pl.BlockSpec((MOE_BLK, D_MODEL), rowo),
        scratch_shapes=[pltpu.VMEM((D_MODEL, 2 * D_FF), BF16), pltpu.VMEM((D_FF, D_MODEL), BF16)])
    return pl.pallas_call(
        _ffn_kernel, grid_spec=grid_spec, out_shape=jax.ShapeDtypeStruct(xs.shape, F32),
        compiler_params=_cp(("arbitrary",)), name="expert_ffn",
    )(blk_e, n_used, xs, w_gu, b_gu.reshape(DEPTH, N_EXPERTS, 1, -1), w_d,
      b_d.reshape(DEPTH, N_EXPERTS, 1, -1))


def _combine_kernel(dest_ref, gate_ref, x_ref, lg_ref, lb_ref, yb_ref, o_ref, buf, sem, *, tm):
    def row_copy(t, k):
        d = dest_ref[t * TOP_K + k]
        return pltpu.make_async_copy(yb_ref.at[pl.ds(d, 1), :], buf.at[k, pl.ds(t, 1), :], sem)

    def issue(t, c):
        for k in range(TOP_K):
            row_copy(t, k).start(priority=k % 2)
        return c

    def drain(t, c):
        for k in range(TOP_K):
            row_copy(t, k).wait()
        return c

    lax.fori_loop(0, tm, issue, 0)
    lax.fori_loop(0, tm, drain, 0)
    gates = gate_ref[...]
    y = buf[0] * gates[:, 0:1]
    for k in range(1, TOP_K):
        y = y + buf[k] * gates[:, k:k + 1]
    o_ref[...] = _ln(DN_ALPHA * x_ref[...] + y, lg_ref[...], lb_ref[...])


def _combine(yb, dest_flat, gates, x, lg, lb, tm=256):
    n = x.shape[0]
    row = lambda i: (i, 0)
    fix = lambda i: (0, 0)
    return pl.pallas_call(
        functools.partial(_combine_kernel, tm=tm), grid=(n // tm,),
        in_specs=[pl.BlockSpec((tm * TOP_K,), lambda i: (i,), memory_space=pltpu.SMEM),
                  pl.BlockSpec((tm, LANES), row), pl.BlockSpec((tm, D_MODEL), row),
                  pl.BlockSpec((1, D_MODEL), fix), pl.BlockSpec((1, D_MODEL), fix),
                  pl.BlockSpec(memory_space=pl.ANY)],
        out_specs=pl.BlockSpec((tm, D_MODEL), row),
        out_shape=jax.ShapeDtypeStruct((n, D_MODEL), F32),
        scratch_shapes=[pltpu.VMEM((TOP_K, tm, D_MODEL), F32), pltpu.SemaphoreType.DMA(())],
        compiler_params=_cp(("arbitrary",)), name="moe_combine")(dest_flat, gates, x, lg, lb, yb)


def _moe(x, wr_pad, br_pad, w_gu, b_gu, w_d, b_d, layer, lg, lb):
    n = x.shape[0]
    idx, gates, rank, cnt = _router(x, wr_pad, br_pad)
    counts = cnt[0, :N_EXPERTS]
    padded = (counts + MOE_BLK - 1) // MOE_BLK * MOE_BLK
    pad_end = jnp.cumsum(padded)
    pad_start = pad_end - padded
    e_sel = idx[:, :TOP_K]
    dest = (pad_start[e_sel] + rank[:, :TOP_K]).reshape(-1).astype(jnp.int32)
    n_blocks = -(-(n * TOP_K) // MOE_BLK) + N_EXPERTS
    blk_e = jnp.sum(pad_end[None, :] <= (jnp.arange(n_blocks) * MOE_BLK)[:, None], axis=1)
    blk_e = jnp.minimum(blk_e, N_EXPERTS - 1).astype(jnp.int32)
    n_used = (pad_end[-1:] // MOE_BLK).astype(jnp.int32)
    xs = _dispatch(x, dest, jnp.zeros((n_blocks * MOE_BLK, D_MODEL), F32))
    yb = _expert_ffn(xs, blk_e, n_used, w_gu, b_gu, w_d, b_d, layer)
    return _combine(yb, dest, gates, x, lg, lb)


def _layer_consts(l, w_in, gdn_conv_w, gdn_a_log, gdn_dt_bias, sgu_norm_g, sgu_norm_b, sgu_w, sgu_b, ts):
    wi = w_in[l]
    w_a = _bf(wi[:, :W_A])
    w_s = _bf(jnp.pad(wi[:, OFF_BB:OFF_CU], ((0, 0), (0, LANES - 2 * B_HEADS))))
    w_c = _bf(wi[:, OFF_CU:])
    lane = jnp.arange(512) // B_DK
    head = jnp.arange(LANES)
    bd = (lane[:, None] == lane[None, :]).astype(BF16)
    eb = (head[:, None] == lane[None, :]).astype(BF16)
    eg = (head[:, None] == lane[None, :] + B_HEADS).astype(BF16)
    alog = jnp.pad(gdn_a_log[l], (B_HEADS, LANES - 2 * B_HEADS)).reshape(1, LANES)
    dtb = jnp.pad(gdn_dt_bias[l], (B_HEADS, LANES - 2 * B_HEADS)).reshape(1, LANES)
    consts = (gdn_conv_w[l], bd, eb, eg, alog, dtb)
    sg, sb = sgu_norm_g[l].reshape(1, -1), sgu_norm_b[l].reshape(1, -1)
    tri = jnp.tril(jnp.ones((C_CHUNK, C_CHUNK), F32))
    ws_p = _bf(sgu_w[l] * tri)
    bx_p = jnp.repeat(sgu_b[l].T, C_CHUNK, axis=1)
    wr_s = jnp.repeat(sgu_w[l][:, :ts, :ts].transpose(1, 2, 0).reshape(ts * ts, C_GROUPS), C_CHUNK, axis=1)
    br_s = jnp.repeat(sgu_b[l][:, :ts].T, C_CHUNK, axis=1)
    return w_a, w_s, w_c, consts, (sg, sb, ws_p, bx_p), (sg, sb, wr_s, br_s)


def kernel(x_prompt, x_sample, mem_prompt, cache_moba_k, cache_moba_v, page_table, state_gdn, state_gdn_conv, cache_mem_k, cache_mem_v, w_in, gdn_conv_w, gdn_a_log, gdn_dt_bias, gdn_norm_g, sgu_norm_g, sgu_norm_b, sgu_w, sgu_b, w_out_a, w_out_b, w_out_c, w_mix_out, w_xq, w_xk, w_xv, w_xo, w_router, b_router, w_gate_up, b_gate_up, w_down, b_down, ln_g, ln_b):
    bp, seq, _ = x_prompt.shape
    db, ts, _ = x_sample.shape
    n_pages = page_table.shape[1]
    n_pool, _, page, _, _ = cache_moba_k.shape
    mem_len = mem_prompt.shape[1]
    np_, ns = bp * seq, db * ts
    ckt = cache_moba_k.transpose(0, 1, 3, 4, 2).reshape(n_pool, DEPTH, A_WIDTH, page)
    cvt = cache_moba_v.transpose(0, 1, 3, 4, 2).reshape(n_pool, DEPTH, A_WIDTH, page)
    pt_flat = page_table.reshape(-1).astype(jnp.int32)
    mem2 = mem_prompt.reshape(bp * mem_len, D_MODEL)
    pad8 = lambda a: jnp.pad(a.reshape(db, ts, -1), ((0, 0), (0, 8 - ts), (0, 0)))
    gdn_npc = 2 if (seq // GDN_CHUNK) % 2 == 0 else 1

    yp = x_prompt.reshape(np_, D_MODEL)
    ys = x_sample.reshape(ns, D_MODEL)
    outs = {k: [] for k in ("kp", "vp", "ks", "vs", "sp", "ss", "cp", "cs", "mk", "mv", "vr")}
    for l in range(DEPTH):
        w_a, w_s, w_c, consts, sgu_p, sgu_s = _layer_consts(
            l, w_in, gdn_conv_w, gdn_a_log, gdn_dt_bias, sgu_norm_g, sgu_norm_b, sgu_w, sgu_b, ts)
        bd = consts[1]
        woa, wob, woc, wmo = _bf(w_out_a[l]), _bf(w_out_b[l]), _bf(w_out_c[l]), _bf(w_mix_out[l])
        wxq, wxo = _bf(w_xq[l]), _bf(w_xo[l])
        lg = [ln_g[l, i].reshape(1, -1) for i in range(3)]
        lb = [ln_b[l, i].reshape(1, -1) for i in range(3)]
        ng = jnp.tile(gdn_norm_g[l], B_HEADS).reshape(1, -1)
        wr_pad = jnp.pad(w_router[l], ((0, 0), (0, LANES - N_EXPERTS)))
        br_pad = jnp.pad(b_router[l], (0, LANES - N_EXPERTS), constant_values=NEG).reshape(1, LANES)

        mk = _matmul(mem2, _bf(w_xk[l]), 512, 512)
        mv = _matmul(mem2, _bf(w_xv[l]), 512, 512)
        hpa = _matmul(yp, w_a, 1024, 512)
        hps = _matmul(yp, w_s, 1024, LANES)
        hpc = _matmul(yp, w_c, 1024, 512)
        qn, kn, vv, bx, gx, yc = _post_prompt(hpa, hps, hpc, consts, sgu_p, bp, seq)
        hpa3 = hpa.reshape(bp, seq, W_A)
        ya = _moba_prompt(hpa3, bp, seq).reshape(np_, A_WIDTH)
        r3 = lambda a: a.reshape(bp, seq, 512)
        s0 = jnp.zeros((bp, HEAD_PAIRS, LANES, LANES), F32)
        yb, sp = _gdn(r3(qn), r3(kn), r3(vv), r3(bx), r3(gx), s0, GDN_CHUNK, bp, gdn_npc)
        x1 = _merge(ya, yb.reshape(np_, 512), hpa, yc, hpc, yp, bd, ng, woa, wob, woc, wmo, lg[0], lb[0])
        qx = _matmul(x1, wxq, 1024, 512)
        ox = _xattn(qx.reshape(bp, seq, D_MODEL), mk.reshape(bp, mem_len, D_MODEL),
                    mv.reshape(bp, mem_len, D_MODEL), 512).reshape(np_, D_MODEL)
        x2p = _proj_res_ln(ox, wxo, x1, lg[1], lb[1], 512)
        outs["kp"].append(hpa3[:, :, 512:1024].reshape(bp, seq, A_HEADS, A_HEAD_DIM))
        outs["vp"].append(hpa3[:, :, 1024:1536].reshape(bp, seq, A_HEADS, A_HEAD_DIM))
        outs["cp"].append(hpa3[:, seq - (CONV_W - 1):, 1536:3072])
        outs["sp"].append(_state_from_pairs(sp))
        outs["mk"].append(mk.reshape(bp, mem_len, X_HEADS, X_HEAD_DIM))
        outs["mv"].append(mv.reshape(bp, mem_len, X_HEADS, X_HEAD_DIM))

        hpa_s = _matmul(ys, w_a, 512, 512)
        hps_s = _matmul(ys, w_s, 512, LANES)
        hpc_s = _matmul(ys, w_c, 512, 512)
        qn_s, kn_s, v_s, bx_s, gx_s, yc_s, vr_s = _post_sample(
            hpa_s, hps_s, hpc_s, state_gdn_conv[:, l], consts, sgu_s, db, ts)
        hpa_s3 = hpa_s.reshape(db, ts, W_A)
        ya_s = _moba_sample(pad8(hpa_s3[:, :, :512]), pad8(hpa_s3[:, :, 512:1024]),
                            pad8(hpa_s3[:, :, 1024:1536]), ckt, cvt, pt_flat, l, db, n_pages, ts)
        ya_s = ya_s[:, :ts].reshape(ns, A_WIDTH)
        gdn_in = [pad8(a) for a in (qn_s, kn_s, v_s, bx_s, gx_s)]
        yb_s, ss = _gdn(*gdn_in, _state_to_pairs(state_gdn[:, l]), 8, 4, 1)
        yb_s = yb_s[:, :ts].reshape(ns, 512)
        x1s = _merge(ya_s, yb_s, hpa_s, yc_s.reshape(ns, 512), hpc_s, ys, bd, ng, woa, wob, woc, wmo,
                     lg[0], lb[0])
        qx_s = _matmul(x1s, wxq, 512, 512)
        ox_s = _xattn_cache(pad8(qx_s), cache_mem_k, cache_mem_v, l)
        ox_s = ox_s[:, :ts].reshape(ns, D_MODEL)
        x2s = _proj_res_ln(ox_s, wxo, x1s, lg[1], lb[1], 512)
        outs["ks"].append(hpa_s3[:, :, 512:1024].reshape(db, ts, A_HEADS, A_HEAD_DIM))
        outs["vs"].append(hpa_s3[:, :, 1024:1536].reshape(db, ts, A_HEADS, A_HEAD_DIM))
        outs["cs"].append(hpa_s3[:, ts - (CONV_W - 1):, 1536:3072])
        outs["ss"].append(_state_from_pairs(ss))
        outs["vr"].append(vr_s.reshape(db, ts, C_WIDTH))

        x3 = _moe(jnp.concatenate([x2p, x2s], axis=0), wr_pad, br_pad, w_gate_up, b_gate_up,
                  w_down, b_down, l, lg[2], lb[2])
        yp, ys = x3[:np_], x3[np_:]

    st = lambda k: jnp.stack(outs[k], axis=1)
    return (yp.reshape(bp, seq, D_MODEL), ys.reshape(db, ts, D_MODEL), st("kp"), st("vp"), st("ks"),
            st("vs"), st("sp"), st("ss"), st("cp"), st("cs"), st("mk"), st("mv"), st("vr"))
```

```python
import functools

import jax
import jax.numpy as jnp
from jax import lax
from jax.experimental import pallas as pl
from jax.experimental.pallas import tpu as pltpu

F32 = jnp.float32
BF16 = jnp.bfloat16

D_MODEL = 1024
DEPTH = 2
A_HEADS = 8
A_HEAD_DIM = 64
A_WIDTH = 512
MOBA_BLOCK = 256
MOBA_TOPK = 3
B_HEADS = 8
B_DK = 64
B_WIDTH = 512
B_CONV_CH = 1536
CONV_W = 4
GDN_CHUNK = 64
C_GROUPS = 4
C_CHUNK = 128
C_WIDTH = 512
X_HEADS = 4
X_HEAD_DIM = 256
N_EXPERTS = 32
TOP_K = 4
D_FF = 1024
SWIGLU_LIMIT = 7.0
SWIGLU_ALPHA = 1.702
DN_ALPHA = (2 * DEPTH) ** 0.25
LN_EPS = 1e-5
RMS_EPS = 1e-6

OFF_BZ = 3072
OFF_BB = 3584
OFF_CU = 3600
P_IN = 7696
W_A = 3584
W_C = P_IN - OFF_CU
LANES = 128
HEAD_PAIRS = B_WIDTH // LANES
MOE_BLK = 256
KV_GROUP = 4
NEG = -1e30
VMEM_LIMIT = 56 * 1024 * 1024


def _cp(sem, vmem=VMEM_LIMIT):
    return pltpu.CompilerParams(dimension_semantics=sem, vmem_limit_bytes=vmem)


def _tile(m, pref):
    if m <= pref:
        return m
    t = pref - pref % 8
    while m % t:
        t -= 8
    return t


def _bf(x):
    return x.astype(BF16)


def _dot(a, b):
    return jnp.dot(_bf(a), _bf(b), preferred_element_type=F32)


_NT = (((1,), (1,)), ((), ()))


def _dot_nt(a, b):
    return lax.dot_general(_bf(a), _bf(b), _NT, preferred_element_type=F32)


def _split(a, n):
    parts, r = [], a
    for _ in range(n):
        p = r.astype(BF16)
        parts.append(p)
        r = r - p.astype(F32)
    return parts


def _dot_x(a, b_bf, n):
    acc = None
    for p in _split(a, n):
        d = jnp.dot(p, b_bf, preferred_element_type=F32)
        acc = d if acc is None else acc + d
    return acc


def _xdot(b_bf, a, n, nt=False):
    acc = None
    for p in _split(a, n):
        if nt:
            d = lax.dot_general(b_bf, p, _NT, preferred_element_type=F32)
        else:
            d = jnp.dot(b_bf, p, preferred_element_type=F32)
        acc = d if acc is None else acc + d
    return acc


def _dot_nt_hl(a, b):
    ah, al = _split(a, 2)
    bh, bl = _split(b, 2)
    f = lambda x, y: lax.dot_general(x, y, _NT, preferred_element_type=F32)
    return f(ah, bh) + f(ah, bl) + f(al, bh)


def _dot_hl(a, b):
    ah, al = _split(a, 2)
    bh, bl = _split(b, 2)
    f = lambda x, y: jnp.dot(x, y, preferred_element_type=F32)
    return f(ah, bh) + f(ah, bl) + f(al, bh)


def _iota(shape, dim):
    return lax.broadcasted_iota(jnp.int32, shape, dim)


def _ln(x, g, b):
    mu = jnp.mean(x, axis=-1, keepdims=True)
    xc = x - mu
    var = jnp.mean(xc * xc, axis=-1, keepdims=True)
    return xc * lax.rsqrt(var + LN_EPS) * g + b


def _softplus(x):
    return jnp.maximum(x, 0.0) + jnp.log(1.0 + jnp.exp(-jnp.abs(x)))


def _top_mask(g, idx, n_idx, k):
    sel = jnp.zeros(g.shape, F32)
    for _ in range(k):
        m = jnp.max(g, axis=-1, keepdims=True)
        first = jnp.min(jnp.where(g == m, idx, float(n_idx)), axis=-1, keepdims=True)
        pick = (idx == first) & (m > -jnp.inf)
        sel = jnp.where(pick, 1.0, sel)
        g = jnp.where(pick, -jnp.inf, g)
    return sel


def _mm_kernel(x_ref, w_ref, o_ref, xb_ref):
    @pl.when(pl.program_id(1) == 0)
    def _():
        xb_ref[...] = _bf(x_ref[...])
    o_ref[...] = jnp.dot(xb_ref[...], w_ref[...], preferred_element_type=F32)


def _matmul(x, w_bf, tm, tn):
    m, k = x.shape
    n = w_bf.shape[1]
    tm, tn = _tile(m, tm), min(tn, n)
    return pl.pallas_call(
        _mm_kernel, grid=(m // tm, n // tn),
        in_specs=[pl.BlockSpec((tm, k), lambda i, j: (i, 0)),
                  pl.BlockSpec((k, tn), lambda i, j: (0, j))],
        out_specs=pl.BlockSpec((tm, tn), lambda i, j: (i, j)),
        out_shape=jax.ShapeDtypeStruct((m, n), F32),
        scratch_shapes=[pltpu.VMEM((tm, k), BF16)],
        compiler_params=_cp(("parallel", "arbitrary")), name="matmul")(x, w_bf)


def _proj_ln_kernel(a_ref, w_ref, r_ref, g_ref, b_ref, o_ref):
    y = jnp.dot(_bf(a_ref[...]), w_ref[...], preferred_element_type=F32)
    o_ref[...] = _ln(DN_ALPHA * r_ref[...] + y, g_ref[...], b_ref[...])


def _proj_res_ln(a, w_bf, res, g, b, tm):
    m, k = a.shape
    tm = _tile(m, tm)
    row = lambda i: (i, 0)
    fix = lambda i: (0, 0)
    return pl.pallas_call(
        _proj_ln_kernel, grid=(m // tm,),
        in_specs=[pl.BlockSpec((tm, k), row), pl.BlockSpec((k, D_MODEL), fix),
                  pl.BlockSpec((tm, D_MODEL), row), pl.BlockSpec((1, D_MODEL), fix),
                  pl.BlockSpec((1, D_MODEL), fix)],
        out_specs=pl.BlockSpec((tm, D_MODEL), row),
        out_shape=jax.ShapeDtypeStruct((m, D_MODEL), F32),
        compiler_params=_cp(("parallel",)), name="proj_res_ln")(a, w_bf, res, g, b)


def _gdn_pre(c, hs, bd, eb, eg, alog, dtb):
    q = c[:, :512]
    k = c[:, 512:1024]
    v = c[:, 1024:]
    qn = q * lax.rsqrt(_dot_x(q * q, bd, 2) + RMS_EPS) * (B_DK ** -0.5)
    kn = k * lax.rsqrt(_dot_x(k * k, bd, 2) + RMS_EPS)
    beta = jax.nn.sigmoid(hs)
    g = -jnp.exp(alog) * _softplus(hs + dtb)
    return qn, kn, v, _dot_x(beta, eb, 3), _dot_x(g, eg, 3)


def _post_p_kernel(bqkv_ref, hs_ref, cuv_ref, cw_ref, bd_ref, eb_ref, eg_ref, alog_ref, dtb_ref,
                   sg_ref, sb_ref, ws_ref, bx_ref,
                   qn_ref, kn_ref, v_ref, beta_ref, g_ref, yc_ref, xe_ref, *, tm):
    @pl.when(pl.program_id(1) == 0)
    def _():
        xe_ref[0:8, :] = jnp.zeros((8, B_CONV_CH), F32)
    xin = bqkv_ref[...]
    xe_ref[8:8 + tm, :] = xin
    cw = cw_ref[...]
    acc = xe_ref[5:5 + tm, :] * cw[0:1]
    acc = acc + xe_ref[6:6 + tm, :] * cw[1:2]
    acc = acc + xe_ref[7:7 + tm, :] * cw[2:3]
    acc = acc + xin * cw[3:4]
    xe_ref[0:8, :] = xe_ref[tm:tm + 8, :]
    c = acc * jax.nn.sigmoid(acc)
    qn, kn, v, bx, gx = _gdn_pre(c, hs_ref[...], bd_ref[...], eb_ref[...], eg_ref[...],
                                 alog_ref[...], dtb_ref[...])
    qn_ref[...] = qn
    kn_ref[...] = kn
    v_ref[...] = v
    beta_ref[...] = bx
    g_ref[...] = gx
    u = jax.nn.gelu(cuv_ref[:, :C_WIDTH])
    vr = _bf(_ln(jax.nn.gelu(cuv_ref[:, C_WIDTH:]), sg_ref[...], sb_ref[...]))
    for ch in range(tm // C_CHUNK):
        rs = slice(ch * C_CHUNK, (ch + 1) * C_CHUNK)
        for gi in range(C_GROUPS):
            cs = slice(gi * LANES, (gi + 1) * LANES)
            mixed = jnp.dot(ws_ref[gi], vr[rs, cs], preferred_element_type=F32) + bx_ref[:, cs]
            yc_ref[rs, cs] = u[rs, cs] * mixed


def _post_prompt(hpa, hps, hpc, consts, sgu, bsz, seq, tm=256):
    n = bsz * seq
    nt = seq // tm
    row = lambda b, t: (b * nt + t, 0)
    fix2 = lambda b, t: (0, 0)
    fix3 = lambda b, t: (0, 0, 0)
    cw, bd, eb, eg, alog, dtb = consts
    sg, sb, ws, bx = sgu
    o512 = jax.ShapeDtypeStruct((n, 512), F32)
    return pl.pallas_call(
        functools.partial(_post_p_kernel, tm=tm), grid=(bsz, nt),
        in_specs=[pl.BlockSpec((tm, B_CONV_CH), lambda b, t: (b * nt + t, 1)),
                  pl.BlockSpec((tm, LANES), row),
                  pl.BlockSpec((tm, 2 * C_WIDTH), row),
                  pl.BlockSpec((CONV_W, B_CONV_CH), fix2),
                  pl.BlockSpec((512, 512), fix2), pl.BlockSpec((LANES, 512), fix2),
                  pl.BlockSpec((LANES, 512), fix2), pl.BlockSpec((1, LANES), fix2),
                  pl.BlockSpec((1, LANES), fix2),
                  pl.BlockSpec((1, C_WIDTH), fix2), pl.BlockSpec((1, C_WIDTH), fix2),
                  pl.BlockSpec((C_GROUPS, C_CHUNK, C_CHUNK), fix3),
                  pl.BlockSpec((C_CHUNK, C_WIDTH), fix2)],
        out_specs=[pl.BlockSpec((tm, 512), row)] * 6,
        out_shape=[o512] * 6,
        scratch_shapes=[pltpu.VMEM((tm + 8, B_CONV_CH), F32)],
        compiler_params=_cp(("parallel", "arbitrary")), name="post_prompt",
    )(hpa, hps, hpc, cw, bd, eb, eg, alog, dtb, sg, sb, ws, bx)


def _post_s_kernel(hpa_ref, hps_ref, hpc_ref, prev_ref, cw_ref, bd_ref, eb_ref, eg_ref, alog_ref,
                   dtb_ref, sg_ref, sb_ref, wr_ref, br_ref,
                   qn_ref, kn_ref, v_ref, beta_ref, g_ref, yc_ref, vr_ref, *, ts):
    cw = cw_ref[...]
    xe = [prev_ref[:, j * B_CONV_CH:(j + 1) * B_CONV_CH] for j in range(CONV_W - 1)]
    xe += [hpa_ref[:, p * W_A + 1536:p * W_A + 3072] for p in range(ts)]
    vrows = []
    for p in range(ts):
        acc = xe[p] * cw[0:1]
        for j in range(1, CONV_W):
            acc = acc + xe[p + j] * cw[j:j + 1]
        c = acc * jax.nn.sigmoid(acc)
        hs = hps_ref[:, p * LANES:(p + 1) * LANES]
        qn, kn, v, bx, gx = _gdn_pre(c, hs, bd_ref[...], eb_ref[...], eg_ref[...],
                                     alog_ref[...], dtb_ref[...])
        ps = slice(p * 512, (p + 1) * 512)
        qn_ref[:, ps] = qn
        kn_ref[:, ps] = kn
        v_ref[:, ps] = v
        beta_ref[:, ps] = bx
        g_ref[:, ps] = gx
        vr = _ln(jax.nn.gelu(hpc_ref[:, p * W_C + C_WIDTH:p * W_C + 2 * C_WIDTH]),
                 sg_ref[...], sb_ref[...])
        vr_ref[:, ps] = vr
        vrows.append(vr)
        mixed = br_ref[p:p + 1, :]
        for j in range(p + 1):
            mixed = mixed + wr_ref[p * ts + j:p * ts + j + 1, :] * vrows[j]
        yc_ref[:, ps] = jax.nn.gelu(hpc_ref[:, p * W_C:p * W_C + C_WIDTH]) * mixed


def _post_sample(hpa, hps, hpc, prev, consts, sgu, db, ts):
    cw, bd, eb, eg, alog, dtb = consts
    sg, sb, wr, br = sgu
    args = (hpa.reshape(db, ts * W_A), hps.reshape(db, ts * LANES), hpc.reshape(db, ts * W_C),
            prev.reshape(db, (CONV_W - 1) * B_CONV_CH), cw, bd, eb, eg, alog, dtb, sg, sb, wr, br)
    o = jax.ShapeDtypeStruct((db, ts * 512), F32)
    return pl.pallas_call(
        functools.partial(_post_s_kernel, ts=ts),
        out_shape=[o] * 7, compiler_params=_cp(None), name="post_sample")(*args)


def _gdn_kernel(q_ref, k_ref, v_ref, bx_ref, gx_ref, s0_ref, o_ref, sf_ref, s_scr,
                *, c_len, bb, npc, n_dbl):
    ci = pl.program_id(1)

    @pl.when(ci == 0)
    def _():
        s_scr[...] = s0_ref[...]

    c2 = 2 * c_len
    ri = _iota((c2, c2), 0)
    cj = _iota((c2, c2), 1)
    same = (ri >= c_len) == (cj >= c_len)
    low = same & (ri >= cj)
    strict = same & (ri > cj)
    eye2 = jnp.where(ri == cj, 1.0, 0.0)
    tril_c = jnp.where(_iota((c_len, c_len), 0) >= _iota((c_len, c_len), 1), 1.0, 0.0).astype(BF16)
    eye_l = jnp.where(_iota((LANES, LANES), 0) == _iota((LANES, LANES), 1), 1.0, 0.0).astype(BF16)
    hsel = jnp.where(_iota((B_HEADS, B_WIDTH), 1) == _iota((B_HEADS, B_WIDTH), 0) * B_DK,
                     1.0, 0.0).astype(BF16)
    place = (_iota((c2, LANES), 1) < B_DK) == (_iota((c2, LANES), 0) < c_len)
    top_rows = _iota((c2, 1), 0) < c_len
    left_cols = _iota((1, c2), 1) < c_len

    chains = [(i, n, p) for i in range(bb) for n in range(npc) for p in range(HEAD_PAIRS)]
    op, decay, glast = {}, {}, {}
    for i in range(bb):
        for n in range(npc):
            rs = slice(n * c_len, (n + 1) * c_len)
            q, k, v = q_ref[i, rs, :], k_ref[i, rs, :], v_ref[i, rs, :]
            bx, gx = bx_ref[i, rs, :], gx_ref[i, rs, :]
            gcb = _xdot(tril_c, gx, 3)
            g2 = jnp.concatenate([gcb, gcb], axis=0)
            gct = _xdot(hsel, g2, 3, nt=True)
            eg = jnp.exp(gcb)
            g_last = gcb[c_len - 1:c_len, :]
            gl = jnp.exp(g_last)
            kb = k * bx
            wide = (k, kb, kb * eg, v * bx, q, q * eg, k * jnp.exp(g_last - gcb))
            for p in range(HEAD_PAIRS):
                ls = slice(p * LANES, (p + 1) * LANES)
                op[(i, n, p)] = [
                    _bf(jnp.where(place, jnp.concatenate([x[:, ls], x[:, ls]], axis=0), 0.0))
                    for x in wide]
                gcol = jnp.where(top_rows, g2[:, p * LANES:p * LANES + 1],
                                 g2[:, p * LANES + B_DK:p * LANES + B_DK + 1])
                grow = jnp.where(left_cols, gct[2 * p:2 * p + 1, :], gct[2 * p + 1:2 * p + 2, :])
                decay[(i, n, p)] = jnp.where(low, jnp.exp(jnp.where(low, gcol - grow, 0.0)), 0.0)
                glast[(i, n, p)] = gl[:, ls]
    mm = lambda x, y: jnp.dot(x, y, preferred_element_type=F32)
    mm_nt = lambda x, y: lax.dot_general(x, y, _NT, preferred_element_type=F32)
    kk = {c: mm_nt(op[c][1], op[c][0]) for c in chains}
    qk = {c: mm_nt(op[c][4], op[c][0]) for c in chains}
    kd_t = {c: _bf(mm_nt(eye_l, op[c][6])) for c in chains}
    a_mat = {c: jnp.where(strict, kk[c] * decay[c], 0.0) for c in chains}
    qk = {c: _bf(qk[c] * decay[c]) for c in chains}
    t_mat = {c: eye2 - a_mat[c] for c in chains}
    pw = a_mat
    for _ in range(n_dbl):
        pwb = {c: _bf(pw[c]) for c in chains}
        pw = {c: mm(pwb[c], pwb[c]) for c in chains}
        tp = {c: mm(_bf(t_mat[c]), _bf(pw[c])) for c in chains}
        t_mat = {c: t_mat[c] + tp[c] for c in chains}
    tb = {c: _bf(t_mat[c]) for c in chains}
    u = {c: mm(tb[c], op[c][3]) for c in chains}
    w = {c: _bf(mm(tb[c], op[c][2])) for c in chains}

    pairs = [(i, p) for i in range(bb) for p in range(HEAD_PAIRS)]
    state = {ip: s_scr[ip[0], ip[1]] for ip in pairs}
    for n in range(npc):
        cn = {ip: (ip[0], n, ip[1]) for ip in pairs}
        sb = {ip: _bf(state[ip]) for ip in pairs}
        ws = {ip: mm(w[cn[ip]], sb[ip]) for ip in pairs}
        qs_ = {ip: mm(op[cn[ip]][5], sb[ip]) for ip in pairs}
        vb = {ip: _bf(u[cn[ip]] - ws[ip]) for ip in pairs}
        ov = {ip: mm(qk[cn[ip]], vb[ip]) for ip in pairs}
        sv = {ip: mm(kd_t[cn[ip]], vb[ip]) for ip in pairs}
        for ip in pairs:
            o = qs_[ip] + ov[ip]
            o_ref[ip[0], n * c_len:(n + 1) * c_len, ip[1] * LANES:(ip[1] + 1) * LANES] = (
                o[:c_len] + o[c_len:])
            state[ip] = state[ip] * glast[cn[ip]] + sv[ip]
    for ip in pairs:
        s_scr[ip[0], ip[1]] = state[ip]

    @pl.when(ci == pl.num_programs(1) - 1)
    def _():
        sf_ref[...] = s_scr[...]


def _gdn(qn, kn, v, bx, gx, s0_bd, c_len, bb, npc):
    bsz, t, _ = qn.shape
    n_dbl = max(c_len.bit_length() - 2, 0)
    tile = npc * c_len
    row = pl.BlockSpec((bb, tile, 512), lambda b, c: (b, c, 0))
    st = pl.BlockSpec((bb, HEAD_PAIRS, LANES, LANES), lambda b, c: (b, 0, 0, 0))
    return pl.pallas_call(
        functools.partial(_gdn_kernel, c_len=c_len, bb=bb, npc=npc, n_dbl=n_dbl),
        grid=(bsz // bb, t // tile),
        in_specs=[row] * 5 + [st], out_specs=[row, st],
        out_shape=[jax.ShapeDtypeStruct((bsz, t, 512), F32),
                   jax.ShapeDtypeStruct((bsz, HEAD_PAIRS, LANES, LANES), F32)],
        scratch_shapes=[pltpu.VMEM((bb, HEAD_PAIRS, LANES, LANES), F32)],
        compiler_params=_cp(("parallel", "arbitrary")), name="gdn",
    )(qn, kn, v, bx, gx, s0_bd)


def _state_to_pairs(s):
    b = s.shape[0]
    s5 = s.reshape(b, HEAD_PAIRS, 2, B_DK, B_DK)
    z = jnp.zeros_like(s5[:, :, 0])
    top = jnp.concatenate([s5[:, :, 0], z], axis=-1)
    bot = jnp.concatenate([z, s5[:, :, 1]], axis=-1)
    return jnp.concatenate([top, bot], axis=-2)


def _state_from_pairs(sp):
    b = sp.shape[0]
    return jnp.stack([sp[:, :, :B_DK, :B_DK], sp[:, :, B_DK:, B_DK:]], axis=2).reshape(
        b, B_HEADS, B_DK, B_DK)


def _moba_p_kernel(q_ref, k_ref, v_ref, o_ref, km_ref, *, nb, nbp, n_sel, grp):
    own = pl.program_id(2)
    tq = MOBA_BLOCK

    @pl.when(own == 0)
    def _():
        km_ref[...] = jnp.zeros((nbp, LANES), F32)
        for n in range(nb):
            km_ref[n:n + 1, :] = jnp.mean(k_ref[0, n * MOBA_BLOCK:(n + 1) * MOBA_BLOCK, :],
                                          axis=0, keepdims=True)

    q2 = q_ref[0]
    first_head = _iota((tq, LANES), 1) < A_HEAD_DIM
    qh = [jnp.where(first_head, q2, 0.0), jnp.where(first_head, 0.0, q2)]
    qhb = [_bf(x * (A_HEAD_DIM ** -0.5)) for x in qh]
    km = km_ref[...]
    blk = _iota((tq, nbp), 1)
    blk_f = blk.astype(F32)
    sels = []
    for h in range(2):
        g = _dot_nt_hl(qh[h], km)
        g = jnp.where(blk < own, g, -jnp.inf)
        sels.append(_top_mask(g, blk_f, nbp, n_sel))

    start = pl.multiple_of(own * MOBA_BLOCK, MOBA_BLOCK)
    k_own = _bf(k_ref[0, pl.ds(start, MOBA_BLOCK), :])
    v_own = _bf(v_ref[0, pl.ds(start, MOBA_BLOCK), :])
    causal = _iota((tq, MOBA_BLOCK), 1) <= _iota((tq, MOBA_BLOCK), 0)
    init = []
    for h in range(2):
        s = lax.dot_general(qhb[h], k_own, _NT, preferred_element_type=F32)
        s = jnp.where(causal, s, NEG)
        m = jnp.max(s, axis=-1, keepdims=True)
        p = jnp.exp(s - m)
        init += [m, jnp.sum(p, axis=-1, keepdims=True),
                 jnp.dot(_bf(p), v_own, preferred_element_type=F32)]

    span = grp * MOBA_BLOCK

    def body(gi, carry):
        st = pl.multiple_of(gi * span, span)
        kg = _bf(k_ref[0, pl.ds(st, span), :])
        vg = _bf(v_ref[0, pl.ds(st, span), :])
        heads = range(2)
        ss = [lax.dot_general(qhb[h], kg, _NT, preferred_element_type=F32) for h in heads]
        for h in heads:
            parts = []
            for c in range(grp):
                selj = jnp.sum(jnp.where(blk == gi * grp + c, sels[h], 0.0), axis=-1, keepdims=True)
                parts.append(jnp.where(selj > 0.0, ss[h][:, c * MOBA_BLOCK:(c + 1) * MOBA_BLOCK], NEG))
            ss[h] = jnp.concatenate(parts, axis=1)
        m_new = [jnp.maximum(carry[3 * h], jnp.max(ss[h], axis=-1, keepdims=True)) for h in heads]
        ps = [jnp.exp(ss[h] - m_new[h]) for h in heads]
        pv = [jnp.dot(_bf(ps[h]), vg, preferred_element_type=F32) for h in heads]
        out = []
        for h in heads:
            alpha = jnp.exp(carry[3 * h] - m_new[h])
            l = carry[3 * h + 1] * alpha + jnp.sum(ps[h], axis=-1, keepdims=True)
            out += [m_new[h], l, carry[3 * h + 2] * alpha + pv[h]]
        return tuple(out)

    n_grp = (own + grp - 1) // grp
    m0, l0, a0, m1, l1, a1 = lax.fori_loop(0, n_grp, body, tuple(init))
    o_ref[0] = jnp.where(first_head, a0 / l0, a1 / l1)


def _moba_prompt(hpa3, bsz, seq):
    nb = seq // MOBA_BLOCK
    nbp = max(8, -(-nb // 8) * 8)
    grp = KV_GROUP if nb % KV_GROUP == 0 else 1
    npair = A_WIDTH // LANES
    return pl.pallas_call(
        functools.partial(_moba_p_kernel, nb=nb, nbp=nbp, n_sel=min(MOBA_TOPK, nb - 1), grp=grp),
        grid=(bsz, npair, nb),
        in_specs=[pl.BlockSpec((1, MOBA_BLOCK, LANES), lambda b, p, i: (b, i, p)),
                  pl.BlockSpec((1, seq, LANES), lambda b, p, i: (b, 0, npair + p)),
                  pl.BlockSpec((1, seq, LANES), lambda b, p, i: (b, 0, 2 * npair + p))],
        out_specs=pl.BlockSpec((1, MOBA_BLOCK, LANES), lambda b, p, i: (b, i, p)),
        out_shape=jax.ShapeDtypeStruct((bsz, seq, A_WIDTH), F32),
        scratch_shapes=[pltpu.VMEM((nbp, LANES), F32)],
        compiler_params=_cp(("parallel", "parallel", "arbitrary")), name="moba_prompt",
    )(hpa3, hpa3, hpa3)


def _moba_s_kernel(pt_ref, q_ref, kn_ref, vn_ref, *refs, n_pages, ts, page):
    kt_pages = refs[:n_pages]
    vt_pages = refs[n_pages:2 * n_pages]
    o_ref, s_ref = refs[2 * n_pages:]
    ppb = MOBA_BLOCK // page
    nb = n_pages // ppb
    rows = A_HEADS * 8
    q8 = q_ref[0]
    lane_head = _iota((8, A_WIDTH), 1) // A_HEAD_DIM
    qbd = jnp.concatenate([jnp.where(lane_head == h, q8, 0.0) for h in range(A_HEADS)], axis=0)
    qbd_b = _bf(qbd)
    scale = A_HEAD_DIM ** -0.5
    col = _iota((A_WIDTH, LANES), 1)
    km = jnp.zeros((A_WIDTH, LANES), F32)
    for n in range(nb):
        tot = None
        for pp in range(ppb):
            sm = jnp.sum(kt_pages[n * ppb + pp][0, 0], axis=1, keepdims=True)
            tot = sm if tot is None else tot + sm
        km = jnp.where(col == n, tot * (1.0 / MOBA_BLOCK), km)
    blk = _iota((rows, LANES), 1)
    g = _dot_hl(qbd, km)
    g = jnp.where(blk < nb, g, -jnp.inf)
    sel = _top_mask(g, blk.astype(F32), LANES, min(MOBA_TOPK, nb))
    s_own = lax.dot_general(qbd_b, _bf(kn_ref[0]), _NT, preferred_element_type=F32) * scale
    kr = _iota((rows, 8), 1)
    qr = _iota((rows, 8), 0) % 8
    s_own = jnp.where((kr <= qr) & (kr < ts), s_own, NEG)
    m = jnp.max(s_own, axis=-1, keepdims=True)
    for pg in range(n_pages):
        s = jnp.dot(qbd_b, _bf(kt_pages[pg][0, 0]), preferred_element_type=F32) * scale
        s = jnp.where(sel[:, pg // ppb:pg // ppb + 1] > 0.0, s, NEG)
        s_ref[:, pg * page:(pg + 1) * page] = s
        m = jnp.maximum(m, jnp.max(s, axis=-1, keepdims=True))
    p_own = jnp.exp(s_own - m)
    l = jnp.sum(p_own, axis=-1, keepdims=True)
    for pg in range(n_pages):
        p = jnp.exp(s_ref[:, pg * page:(pg + 1) * page] - m)
        s_ref[:, pg * page:(pg + 1) * page] = p
        l = l + jnp.sum(p, axis=-1, keepdims=True)
    inv = 1.0 / l
    o = jnp.dot(_bf(p_own * inv), _bf(vn_ref[0]), preferred_element_type=F32)
    for pg in range(n_pages):
        p = s_ref[:, pg * page:(pg + 1) * page] * inv
        o = o + lax.dot_general(_bf(p), _bf(vt_pages[pg][0, 0]), _NT, preferred_element_type=F32)
    out = jnp.zeros((8, A_WIDTH), F32)
    for h in range(A_HEADS):
        out = out + jnp.where(lane_head == h, o[h * 8:(h + 1) * 8, :], 0.0)
    o_ref[0] = out


def _moba_sample(q8, kn8, vn8, cache_kt, cache_vt, pt_flat, layer, db, n_pages, ts):
    page = cache_kt.shape[3]
    tok = pl.BlockSpec((1, 8, A_WIDTH), lambda b, pt: (b, 0, 0))

    def page_spec(p):
        return pl.BlockSpec((1, 1, A_WIDTH, page), lambda b, pt: (pt[b * n_pages + p], layer, 0, 0))

    specs = [page_spec(p) for p in range(n_pages)]
    grid_spec = pltpu.PrefetchScalarGridSpec(
        num_scalar_prefetch=1, grid=(db,),
        in_specs=[tok, tok, tok] + specs + specs,
        out_specs=tok,
        scratch_shapes=[pltpu.VMEM((A_HEADS * 8, n_pages * page), F32)])
    return pl.pallas_call(
        functools.partial(_moba_s_kernel, n_pages=n_pages, ts=ts, page=page),
        grid_spec=grid_spec, out_shape=jax.ShapeDtypeStruct((db, 8, A_WIDTH), F32),
        compiler_params=_cp(("parallel",)), name="moba_sample",
    )(pt_flat, q8, kn8, vn8, *([cache_kt] * n_pages), *([cache_vt] * n_pages))


def _merge_kernel(ya_ref, yb_ref, z_ref, yc_ref, g0_ref, g1_ref, g2_ref, x_ref, bd_ref, ng_ref,
                  woa_ref, wob_ref, woc_ref, wmo_ref, lg_ref, lb_ref, o_ref):
    f = lambda y, w: jnp.dot(_bf(y), w[...], preferred_element_type=F32)
    yb = yb_ref[...]
    z = z_ref[...]
    ms = _dot_x(yb * yb, bd_ref[...], 2) * (1.0 / B_DK)
    yb = yb * lax.rsqrt(ms + RMS_EPS) * ng_ref[...] * (z * jax.nn.sigmoid(z))
    merged = jax.nn.sigmoid(g0_ref[...]) * f(ya_ref[...], woa_ref)
    merged = merged + jax.nn.sigmoid(g1_ref[...]) * f(yb, wob_ref)
    merged = merged + jax.nn.sigmoid(g2_ref[...]) * f(yc_ref[...], woc_ref)
    y = jnp.dot(_bf(merged), wmo_ref[...], preferred_element_type=F32)
    o_ref[...] = _ln(DN_ALPHA * x_ref[...] + y, lg_ref[...], lb_ref[...])


def _merge(ya, yb, hpa, yc, hpc, x, bd, ng, woa, wob, woc, wmo, lg, lb, tm=256):
    m = x.shape[0]
    tm = _tile(m, tm)
    row = lambda i: (i, 0)
    fix = lambda i: (0, 0)
    y512 = pl.BlockSpec((tm, 512), row)
    gate = lambda j: pl.BlockSpec((tm, D_MODEL), lambda i: (i, j))
    wsm = pl.BlockSpec((512, D_MODEL), fix)
    vec = pl.BlockSpec((1, D_MODEL), fix)
    return pl.pallas_call(
        _merge_kernel, grid=(m // tm,),
        in_specs=[y512, y512, pl.BlockSpec((tm, 512), lambda i: (i, OFF_BZ // 512)), y512,
                  gate(1), gate(2), gate(3), pl.BlockSpec((tm, D_MODEL), row),
                  pl.BlockSpec((512, 512), fix), pl.BlockSpec((1, 512), fix),
                  wsm, wsm, wsm, pl.BlockSpec((D_MODEL, D_MODEL), fix), vec, vec],
        out_specs=pl.BlockSpec((tm, D_MODEL), row),
        out_shape=jax.ShapeDtypeStruct((m, D_MODEL), F32),
        compiler_params=_cp(("parallel",)), name="merge",
    )(ya, yb, hpa, yc, hpc, hpc, hpc, x, bd, ng, woa, wob, woc, wmo, lg, lb)


def _attend(q, k, v):
    s = _dot_nt(q, k) * (X_HEAD_DIM ** -0.5)
    p = jnp.exp(s - jnp.max(s, axis=-1, keepdims=True))
    p = p / jnp.sum(p, axis=-1, keepdims=True)
    return _dot(p, v)


def _xattn_kernel(q_ref, k_ref, v_ref, o_ref):
    for h in range(X_HEADS):
        cs = slice(h * X_HEAD_DIM, (h + 1) * X_HEAD_DIM)
        o_ref[0, :, cs] = _attend(q_ref[0, :, cs], k_ref[0, :, cs], v_ref[0, :, cs])


def _xattn(q3, k3, v3, tm):
    g, r, _ = q3.shape
    tm = _tile(r, tm)
    mem = k3.shape[1]
    qs = pl.BlockSpec((1, tm, D_MODEL), lambda i, t: (i, t, 0))
    ms = pl.BlockSpec((1, mem, D_MODEL), lambda i, t: (i, 0, 0))
    return pl.pallas_call(
        _xattn_kernel, grid=(g, r // tm), in_specs=[qs, ms, ms], out_specs=qs,
        out_shape=jax.ShapeDtypeStruct(q3.shape, F32),
        compiler_params=_cp(("parallel", "parallel")), name="xattn")(q3, k3, v3)


def _xattn_cache_kernel(q_ref, k_hbm, v_hbm, o_ref, kbuf, vbuf, sem, *, nseq, layer):
    units = [(g, h) for g in range(nseq) for h in range(X_HEADS)]
    base = pl.program_id(0) * nseq

    def head_copies(g, h):
        return (pltpu.make_async_copy(k_hbm.at[base + g, layer, :, h, :], kbuf.at[g, h], sem.at[0]),
                pltpu.make_async_copy(v_hbm.at[base + g, layer, :, h, :], vbuf.at[g, h], sem.at[1]))

    for u in units:
        for c in head_copies(*u):
            c.start()
    for u in units:
        for c in head_copies(*u):
            c.wait()
    cs = lambda h: slice(h * X_HEAD_DIM, (h + 1) * X_HEAD_DIM)
    scale = X_HEAD_DIM ** -0.5
    s = {u: _dot_nt(q_ref[u[0], :, cs(u[1])] * scale, kbuf[u[0], u[1]]) for u in units}
    p = {u: jnp.exp(s[u] - jnp.max(s[u], axis=-1, keepdims=True)) for u in units}
    p = {u: p[u] / jnp.sum(p[u], axis=-1, keepdims=True) for u in units}
    o = {u: _dot(p[u], vbuf[u[0], u[1]]) for u in units}
    for g, h in units:
        o_ref[g, :, cs(h)] = o[(g, h)]


def _xattn_cache(q3, cache_k, cache_v, layer, nseq=4):
    g = q3.shape[0]
    mem = cache_k.shape[2]
    qs = pl.BlockSpec((nseq, 8, D_MODEL), lambda i: (i, 0, 0))
    anyspec = pl.BlockSpec(memory_space=pl.ANY)
    buf = pltpu.VMEM((nseq, X_HEADS, mem, X_HEAD_DIM), F32)
    return pl.pallas_call(
        functools.partial(_xattn_cache_kernel, nseq=nseq, layer=layer), grid=(g // nseq,),
        in_specs=[qs, anyspec, anyspec], out_specs=qs,
        out_shape=jax.ShapeDtypeStruct(q3.shape, F32),
        scratch_shapes=[buf, buf, pltpu.SemaphoreType.DMA((2,))],
        compiler_params=_cp(("arbitrary",)), name="xattn_cache")(q3, cache_k, cache_v)


def _router_kernel(x_ref, wr_ref, br_ref, tri_ref, idx_ref, gate_ref, rank_ref, cnt_ref, carry_ref):
    @pl.when(pl.program_id(0) == 0)
    def _():
        carry_ref[...] = jnp.zeros(carry_ref.shape, F32)

    logits = _dot_hl(x_ref[...], wr_ref[...]) + br_ref[...]
    tm = logits.shape[0]
    lane = _iota((tm, LANES), 1)
    lane_f = lane.astype(F32)
    lg = logits
    onehots, tops, firsts = [], [], []
    for _ in range(TOP_K):
        m = jnp.max(lg, axis=-1, keepdims=True)
        first = jnp.min(jnp.where(lg == m, lane_f, float(LANES)), axis=-1, keepdims=True)
        pick = lane_f == first
        lg = jnp.where(pick, -jnp.inf, lg)
        onehots.append(jnp.where(pick, 1.0, 0.0))
        tops.append(m)
        firsts.append(first)
    es = [jnp.exp(t - tops[0]) for t in tops]
    den = es[0] + es[1] + es[2] + es[3]
    all_hot = onehots[0] + onehots[1] + onehots[2] + onehots[3]
    before = jnp.dot(tri_ref[...], _bf(all_hot), preferred_element_type=F32) + carry_ref[...]
    idx_o = jnp.zeros((tm, LANES), F32)
    gate_o = jnp.zeros((tm, LANES), F32)
    rank_o = jnp.zeros((tm, LANES), F32)
    for k in range(TOP_K):
        rk = jnp.sum(onehots[k] * before, axis=-1, keepdims=True)
        idx_o = jnp.where(lane == k, firsts[k], idx_o)
        gate_o = jnp.where(lane == k, es[k] / den, gate_o)
        rank_o = jnp.where(lane == k, rk, rank_o)
    idx_ref[...] = idx_o.astype(jnp.int32)
    gate_ref[...] = gate_o
    rank_ref[...] = rank_o.astype(jnp.int32)
    carry_ref[...] = carry_ref[...] + jnp.sum(all_hot, axis=0, keepdims=True)
    cnt_ref[...] = carry_ref[...].astype(jnp.int32)


def _router(x, wr_pad, br_pad, tm=512):
    n = x.shape[0]
    tm = _tile(n, tm)
    tri = jnp.tril(jnp.ones((tm, tm), F32), -1).astype(BF16)
    row = lambda i: (i, 0)
    fix = lambda i: (0, 0)
    o = pl.BlockSpec((tm, LANES), row)
    return pl.pallas_call(
        _router_kernel, grid=(n // tm,),
        in_specs=[pl.BlockSpec((tm, D_MODEL), row), pl.BlockSpec((D_MODEL, LANES), fix),
                  pl.BlockSpec((1, LANES), fix), pl.BlockSpec((tm, tm), fix)],
        out_specs=[o, o, o, pl.BlockSpec((1, LANES), fix)],
        out_shape=[jax.ShapeDtypeStruct((n, LANES), jnp.int32), jax.ShapeDtypeStruct((n, LANES), F32),
                   jax.ShapeDtypeStruct((n, LANES), jnp.int32),
                   jax.ShapeDtypeStruct((1, LANES), jnp.int32)],
        scratch_shapes=[pltpu.VMEM((1, LANES), F32)],
        compiler_params=_cp(("arbitrary",)), name="router")(x, wr_pad, br_pad, tri)


def _dispatch_kernel(dest_ref, x_ref, xs_in_ref, xs_ref, sem, *, tm):
    del xs_in_ref

    def row_copy(t, k):
        d = dest_ref[t * TOP_K + k]
        return pltpu.make_async_copy(x_ref.at[pl.ds(t, 1), :], xs_ref.at[pl.ds(d, 1), :], sem)

    def issue(t, c):
        for k in range(TOP_K):
            row_copy(t, k).start(priority=k % 2)
        return c

    def drain(t, c):
        for k in range(TOP_K):
            row_copy(t, k).wait()
        return c

    lax.fori_loop(0, tm, issue, 0)
    lax.fori_loop(0, tm, drain, 0)


def _dispatch(x, dest_flat, xs_zero, tm=256):
    n = x.shape[0]
    return pl.pallas_call(
        functools.partial(_dispatch_kernel, tm=tm), grid=(n // tm,),
        in_specs=[pl.BlockSpec((tm * TOP_K,), lambda i: (i,), memory_space=pltpu.SMEM),
                  pl.BlockSpec((tm, D_MODEL), lambda i: (i, 0)),
                  pl.BlockSpec(memory_space=pl.ANY)],
        out_specs=pl.BlockSpec(memory_space=pl.ANY),
        out_shape=jax.ShapeDtypeStruct(xs_zero.shape, F32),
        scratch_shapes=[pltpu.SemaphoreType.DMA(())],
        input_output_aliases={2: 0},
        compiler_params=_cp(("arbitrary",)), name="moe_dispatch")(dest_flat, x, xs_zero)


def _ffn_kernel(be_ref, nu_ref, x_ref, wgu_ref, bgu_ref, wd_ref, bd_ref, o_ref, wgu_b, wd_b):
    i = pl.program_id(0)
    prev = be_ref[jnp.maximum(i - 1, 0)]
    used = i < nu_ref[0]

    @pl.when(used & ((i == 0) | (be_ref[i] != prev)))
    def _():
        wgu_b[...] = _bf(wgu_ref[0, 0])
        wd_b[...] = _bf(wd_ref[0, 0])

    @pl.when(used)
    def _():
        hgu = jnp.dot(_bf(x_ref[...]), wgu_b[...], preferred_element_type=F32) + bgu_ref[0, 0]
        gate = jnp.minimum(hgu[:, :D_FF], SWIGLU_LIMIT)
        up = jnp.clip(hgu[:, D_FF:], -SWIGLU_LIMIT, SWIGLU_LIMIT)
        act = (up + 1.0) * gate * jax.nn.sigmoid(gate * SWIGLU_ALPHA)
        o_ref[...] = jnp.dot(_bf(act), wd_b[...], preferred_element_type=F32) + bd_ref[0, 0]

    @pl.when(jnp.logical_not(used))
    def _():
        o_ref[...] = jnp.zeros(o_ref.shape, F32)


def _expert_ffn(xs, blk_e, n_used, w_gu, b_gu, w_d, b_d, layer):
    n_blocks = xs.shape[0] // MOE_BLK
    rowi = lambda i, be, nu: (jnp.minimum(i, nu[0] - 1), 0)
    rowo = lambda i, be, nu: (i, 0)
    exp4 = lambda i, be, nu: (layer, be[i], 0, 0)
    grid_spec = pltpu.PrefetchScalarGridSpec(
        num_scalar_prefetch=2, grid=(n_blocks,),
        in_specs=[pl.BlockSpec((MOE_BLK, D_MODEL), rowi),
                  pl.BlockSpec((1, 1, D_MODEL, 2 * D_FF), exp4), pl.BlockSpec((1, 1, 1, 2 * D_FF), exp4),
                  pl.BlockSpec((1, 1, D_FF, D_MODEL), exp4), pl.BlockSpec((1, 1, 1, D_MODEL), exp4)],
        out_specs=pl.BlockSpec((MOE_BLK, D_MODEL), rowo),
        scratch_shapes=[pltpu.VMEM((D_MODEL, 2 * D_FF), BF16), pltpu.VMEM((D_FF, D_MODEL), BF16)])
    return pl.pallas_call(
        _ffn_kernel, grid_spec=grid_spec, out_shape=jax.ShapeDtypeStruct(xs.shape, F32),
        compiler_params=_cp(("arbitrary",)), name="expert_ffn",
    )(blk_e, n_used, xs, w_gu, b_gu.reshape(DEPTH, N_EXPERTS, 1, -1), w_d,
      b_d.reshape(DEPTH, N_EXPERTS, 1, -1))


def _combine_kernel(dest_ref, gate_ref, x_ref, lg_ref, lb_ref, yb_ref, o_ref, buf, sem, *, tm):
    def row_copy(t, k):
        d = dest_ref[t * TOP_K + k]
        return pltpu.make_async_copy(yb_ref.at[pl.ds(d, 1), :], buf.at[k, pl.ds(t, 1), :], sem)

    def issue(t, c):
        for k in range(TOP_K):
            row_copy(t, k).start(priority=k % 2)
        return c

    def drain(t, c):
        for k in range(TOP_K):
            row_copy(t, k).wait()
        return c

    lax.fori_loop(0, tm, issue, 0)
    lax.fori_loop(0, tm, drain, 0)
    gates = gate_ref[...]
    y = buf[0] * gates[:, 0:1]
    for k in range(1, TOP_K):
        y = y + buf[k] * gates[:, k:k + 1]
    o_ref[...] = _ln(DN_ALPHA * x_ref[...] + y, lg_ref[...], lb_ref[...])


def _combine(yb, dest_flat, gates, x, lg, lb, tm=256):
    n = x.shape[0]
    row = lambda i: (i, 0)
    fix = lambda i: (0, 0)
    return pl.pallas_call(
        functools.partial(_combine_kernel, tm=tm), grid=(n // tm,),
        in_specs=[pl.BlockSpec((tm * TOP_K,), lambda i: (i,), memory_space=pltpu.SMEM),
                  pl.BlockSpec((tm, LANES), row), pl.BlockSpec((tm, D_MODEL), row),
                  pl.BlockSpec((1, D_MODEL), fix), pl.BlockSpec((1, D_MODEL), fix),
                  pl.BlockSpec(memory_space=pl.ANY)],
        out_specs=pl.BlockSpec((tm, D_MODEL), row),
        out_shape=jax.ShapeDtypeStruct((n, D_MODEL), F32),
        scratch_shapes=[pltpu.VMEM((TOP_K, tm, D_MODEL), F32), pltpu.SemaphoreType.DMA(())],
        compiler_params=_cp(("arbitrary",)), name="moe_combine")(dest_flat, gates, x, lg, lb, yb)


def _moe(x, wr_pad, br_pad, w_gu, b_gu, w_d, b_d, layer, lg, lb):
    n = x.shape[0]
    idx, gates, rank, cnt = _router(x, wr_pad, br_pad)
    counts = cnt[0, :N_EXPERTS]
    padded = (counts + MOE_BLK - 1) // MOE_BLK * MOE_BLK
    pad_end = jnp.cumsum(padded)
    pad_start = pad_end - padded
    e_sel = idx[:, :TOP_K]
    dest = (pad_start[e_sel] + rank[:, :TOP_K]).reshape(-1).astype(jnp.int32)
    n_blocks = -(-(n * TOP_K) // MOE_BLK) + N_EXPERTS
    blk_e = jnp.sum(pad_end[None, :] <= (jnp.arange(n_blocks) * MOE_BLK)[:, None], axis=1)
    blk_e = jnp.minimum(blk_e, N_EXPERTS - 1).astype(jnp.int32)
    n_used = (pad_end[-1:] // MOE_BLK).astype(jnp.int32)
    xs = _dispatch(x, dest, jnp.zeros((n_blocks * MOE_BLK, D_MODEL), F32))
    yb = _expert_ffn(xs, blk_e, n_used, w_gu, b_gu, w_d, b_d, layer)
    return _combine(yb, dest, gates, x, lg, lb)


def _layer_consts(l, w_in, gdn_conv_w, gdn_a_log, gdn_dt_bias, sgu_norm_g, sgu_norm_b, sgu_w, sgu_b, ts):
    wi = w_in[l]
    w_a = _bf(wi[:, :W_A])
    w_s = _bf(jnp.pad(wi[:, OFF_BB:OFF_CU], ((0, 0), (0, LANES - 2 * B_HEADS))))
    w_c = _bf(wi[:, OFF_CU:])
    lane = jnp.arange(512) // B_DK
    head = jnp.arange(LANES)
    bd = (lane[:, None] == lane[None, :]).astype(BF16)
    eb = (head[:, None] == lane[None, :]).astype(BF16)
    eg = (head[:, None] == lane[None, :] + B_HEADS).astype(BF16)
    alog = jnp.pad(gdn_a_log[l], (B_HEADS, LANES - 2 * B_HEADS)).reshape(1, LANES)
    dtb = jnp.pad(gdn_dt_bias[l], (B_HEADS, LANES - 2 * B_HEADS)).reshape(1, LANES)
    consts = (gdn_conv_w[l], bd, eb, eg, alog, dtb)
    sg, sb = sgu_norm_g[l].reshape(1, -1), sgu_norm_b[l].reshape(1, -1)
    tri = jnp.tril(jnp.ones((C_CHUNK, C_CHUNK), F32))
    ws_p = _bf(sgu_w[l] * tri)
    bx_p = jnp.repeat(sgu_b[l].T, C_CHUNK, axis=1)
    wr_s = jnp.repeat(sgu_w[l][:, :ts, :ts].transpose(1, 2, 0).reshape(ts * ts, C_GROUPS), C_CHUNK, axis=1)
    br_s = jnp.repeat(sgu_b[l][:, :ts].T, C_CHUNK, axis=1)
    return w_a, w_s, w_c, consts, (sg, sb, ws_p, bx_p), (sg, sb, wr_s, br_s)


def kernel(x_prompt, x_sample, mem_prompt, cache_moba_k, cache_moba_v, page_table, state_gdn, state_gdn_conv, cache_mem_k, cache_mem_v, w_in, gdn_conv_w, gdn_a_log, gdn_dt_bias, gdn_norm_g, sgu_norm_g, sgu_norm_b, sgu_w, sgu_b, w_out_a, w_out_b, w_out_c, w_mix_out, w_xq, w_xk, w_xv, w_xo, w_router, b_router, w_gate_up, b_gate_up, w_down, b_down, ln_g, ln_b):
    bp, seq, _ = x_prompt.shape
    db, ts, _ = x_sample.shape
    n_pages = page_table.shape[1]
    n_pool, _, page, _, _ = cache_moba_k.shape
    mem_len = mem_prompt.shape[1]
    np_, ns = bp * seq, db * ts
    ckt = cache_moba_k.transpose(0, 1, 3, 4, 2).reshape(n_pool, DEPTH, A_WIDTH, page)
    cvt = cache_moba_v.transpose(0, 1, 3, 4, 2).reshape(n_pool, DEPTH, A_WIDTH, page)
    pt_flat = page_table.reshape(-1).astype(jnp.int32)
    mem2 = mem_prompt.reshape(bp * mem_len, D_MODEL)
    pad8 = lambda a: jnp.pad(a.reshape(db, ts, -1), ((0, 0), (0, 8 - ts), (0, 0)))
    gdn_npc = 2 if (seq // GDN_CHUNK) % 2 == 0 else 1

    yp = x_prompt.reshape(np_, D_MODEL)
    ys = x_sample.reshape(ns, D_MODEL)
    outs = {k: [] for k in ("kp", "vp", "ks", "vs", "sp", "ss", "cp", "cs", "mk", "mv", "vr")}
    for l in range(DEPTH):
        w_a, w_s, w_c, consts, sgu_p, sgu_s = _layer_consts(
            l, w_in, gdn_conv_w, gdn_a_log, gdn_dt_bias, sgu_norm_g, sgu_norm_b, sgu_w, sgu_b, ts)
        bd = consts[1]
        woa, wob, woc, wmo = _bf(w_out_a[l]), _bf(w_out_b[l]), _bf(w_out_c[l]), _bf(w_mix_out[l])
        wxq, wxo = _bf(w_xq[l]), _bf(w_xo[l])
        lg = [ln_g[l, i].reshape(1, -1) for i in range(3)]
        lb = [ln_b[l, i].reshape(1, -1) for i in range(3)]
        ng = jnp.tile(gdn_norm_g[l], B_HEADS).reshape(1, -1)
        wr_pad = jnp.pad(w_router[l], ((0, 0), (0, LANES - N_EXPERTS)))
        br_pad = jnp.pad(b_router[l], (0, LANES - N_EXPERTS), constant_values=NEG).reshape(1, LANES)

        mk = _matmul(mem2, _bf(w_xk[l]), 512, 512)
        mv = _matmul(mem2, _bf(w_xv[l]), 512, 512)
        hpa = _matmul(yp, w_a, 1024, 512)
        hps = _matmul(yp, w_s, 1024, LANES)
        hpc = _matmul(yp, w_c, 1024, 512)
        qn, kn, vv, bx, gx, yc = _post_prompt(hpa, hps, hpc, consts, sgu_p, bp, seq)
        hpa3 = hpa.reshape(bp, seq, W_A)
        ya = _moba_prompt(hpa3, bp, seq).reshape(np_, A_WIDTH)
        r3 = lambda a: a.reshape(bp, seq, 512)
        s0 = jnp.zeros((bp, HEAD_PAIRS, LANES, LANES), F32)
        yb, sp = _gdn(r3(qn), r3(kn), r3(vv), r3(bx), r3(gx), s0, GDN_CHUNK, bp, gdn_npc)
        x1 = _merge(ya, yb.reshape(np_, 512), hpa, yc, hpc, yp, bd, ng, woa, wob, woc, wmo, lg[0], lb[0])
        qx = _matmul(x1, wxq, 1024, 512)
        ox = _xattn(qx.reshape(bp, seq, D_MODEL), mk.reshape(bp, mem_len, D_MODEL),
                    mv.reshape(bp, mem_len, D_MODEL), 512).reshape(np_, D_MODEL)
        x2p = _proj_res_ln(ox, wxo, x1, lg[1], lb[1], 512)
        outs["kp"].append(hpa3[:, :, 512:1024].reshape(bp, seq, A_HEADS, A_HEAD_DIM))
        outs["vp"].append(hpa3[:, :, 1024:1536].reshape(bp, seq, A_HEADS, A_HEAD_DIM))
        outs["cp"].append(hpa3[:, seq - (CONV_W - 1):, 1536:3072])
        outs["sp"].append(_state_from_pairs(sp))
        outs["mk"].append(mk.reshape(bp, mem_len, X_HEADS, X_HEAD_DIM))
        outs["mv"].append(mv.reshape(bp, mem_len, X_HEADS, X_HEAD_DIM))

        hpa_s = _matmul(ys, w_a, 512, 512)
        hps_s = _matmul(ys, w_s, 512, LANES)
        hpc_s = _matmul(ys, w_c, 512, 512)
        qn_s, kn_s, v_s, bx_s, gx_s, yc_s, vr_s = _post_sample(
            hpa_s, hps_s, hpc_s, state_gdn_conv[:, l], consts, sgu_s, db, ts)
        hpa_s3 = hpa_s.reshape(db, ts, W_A)
        ya_s = _moba_sample(pad8(hpa_s3[:, :, :512]), pad8(hpa_s3[:, :, 512:1024]),
                            pad8(hpa_s3[:, :, 1024:1536]), ckt, cvt, pt_flat, l, db, n_pages, ts)
        ya_s = ya_s[:, :ts].reshape(ns, A_WIDTH)
        gdn_in = [pad8(a) for a in (qn_s, kn_s, v_s, bx_s, gx_s)]
        yb_s, ss = _gdn(*gdn_in, _state_to_pairs(state_gdn[:, l]), 8, 4, 1)
        yb_s = yb_s[:, :ts].reshape(ns, 512)
        x1s = _merge(ya_s, yb_s, hpa_s, yc_s.reshape(ns, 512), hpc_s, ys, bd, ng, woa, wob, woc, wmo,
                     lg[0], lb[0])
        qx_s = _matmul(x1s, wxq, 512, 512)
        ox_s = _xattn_cache(pad8(qx_s), cache_mem_k, cache_mem_v, l)
        ox_s = ox_s[:, :ts].reshape(ns, D_MODEL)
        x2s = _proj_res_ln(ox_s, wxo, x1s, lg[1], lb[1], 512)
        outs["ks"].append(hpa_s3[:, :, 512:1024].reshape(db, ts, A_HEADS, A_HEAD_DIM))
        outs["vs"].append(hpa_s3[:, :, 1024:1536].reshape(db, ts, A_HEADS, A_HEAD_DIM))
        outs["cs"].append(hpa_s3[:, ts - (CONV_W - 1):, 1536:3072])
        outs["ss"].append(_state_from_pairs(ss))
        outs["vr"].append(vr_s.reshape(db, ts, C_WIDTH))

        x3 = _moe(jnp.concatenate([x2p, x2s], axis=0), wr_pad, br_pad, w_gate_up, b_gate_up,
                  w_down, b_down, l, lg[2], lb[2])
        yp, ys = x3[:np_], x3[np_:]

    st = lambda k: jnp.stack(outs[k], axis=1)
    return (yp.reshape(bp, seq, D_MODEL), ys.reshape(db, ts, D_MODEL), st("kp"), st("vp"), st("ks"),
            st("vs"), st("sp"), st("ss"), st("cp"), st("cs"), st("mk"), st("mv"), st("vr"))
```

```python
import functools

import jax
import jax.numpy as jnp
from jax import lax
from jax.experimental import pallas as pl
from jax.experimental.pallas import tpu as pltpu

F32 = jnp.float32
BF16 = jnp.bfloat16

D_MODEL = 1024
DEPTH = 2
A_HEADS = 8
A_HEAD_DIM = 64
A_WIDTH = 512
MOBA_BLOCK = 256
MOBA_TOPK = 3
B_HEADS = 8
B_DK = 64
B_WIDTH = 512
B_CONV_CH = 1536
CONV_W = 4
GDN_CHUNK = 64
C_GROUPS = 4
C_CHUNK = 128
C_WIDTH = 512
X_HEADS = 4
X_HEAD_DIM = 256
N_EXPERTS = 32
TOP_K = 4
D_FF = 1024
SWIGLU_LIMIT = 7.0
SWIGLU_ALPHA = 1.702
DN_ALPHA = (2 * DEPTH) ** 0.25
LN_EPS = 1e-5
RMS_EPS = 1e-6

OFF_BZ = 3072
OFF_BB = 3584
OFF_CU = 3600
P_IN = 7696
W_A = 3584
W_C = P_IN - OFF_CU
LANES = 128
HEAD_PAIRS = B_WIDTH // LANES
MOE_BLK = 256
KV_GROUP = 4
NEG = -1e30
VMEM_LIMIT = 56 * 1024 * 1024


def _cp(sem, vmem=VMEM_LIMIT):
    return pltpu.CompilerParams(dimension_semantics=sem, vmem_limit_bytes=vmem)


def _tile(m, pref):
    if m <= pref:
        return m
    t = pref - pref % 8
    while m % t:
        t -= 8
    return t


def _bf(x):
    return x.astype(BF16)


def _dot(a, b):
    return jnp.dot(_bf(a), _bf(b), preferred_element_type=F32)


_NT = (((1,), (1,)), ((), ()))


def _dot_nt(a, b):
    return lax.dot_general(_bf(a), _bf(b), _NT, preferred_element_type=F32)


def _split(a, n):
    parts, r = [], a
    for _ in range(n):
        p = r.astype(BF16)
        parts.append(p)
        r = r - p.astype(F32)
    return parts


def _dot_x(a, b_bf, n):
    acc = None
    for p in _split(a, n):
        d = jnp.dot(p, b_bf, preferred_element_type=F32)
        acc = d if acc is None else acc + d
    return acc


def _xdot(b_bf, a, n, nt=False):
    acc = None
    for p in _split(a, n):
        if nt:
            d = lax.dot_general(b_bf, p, _NT, preferred_element_type=F32)
        else:
            d = jnp.dot(b_bf, p, preferred_element_type=F32)
        acc = d if acc is None else acc + d
    return acc


def _dot_nt_hl(a, b):
    ah, al = _split(a, 2)
    bh, bl = _split(b, 2)
    f = lambda x, y: lax.dot_general(x, y, _NT, preferred_element_type=F32)
    return f(ah, bh) + f(ah, bl) + f(al, bh)


def _dot_hl(a, b):
    ah, al = _split(a, 2)
    bh, bl = _split(b, 2)
    f = lambda x, y: jnp.dot(x, y, preferred_element_type=F32)
    return f(ah, bh) + f(ah, bl) + f(al, bh)


def _iota(shape, dim):
    return lax.broadcasted_iota(jnp.int32, shape, dim)


def _ln(x, g, b):
    mu = jnp.mean(x, axis=-1, keepdims=True)
    xc = x - mu
    var = jnp.mean(xc * xc, axis=-1, keepdims=True)
    return xc * lax.rsqrt(var + LN_EPS) * g + b


def _softplus(x):
    return jnp.maximum(x, 0.0) + jnp.log(1.0 + jnp.exp(-jnp.abs(x)))


def _top_mask(g, idx, n_idx, k):
    sel = jnp.zeros(g.shape, F32)
    for _ in range(k):
        m = jnp.max(g, axis=-1, keepdims=True)
        first = jnp.min(jnp.where(g == m, idx, float(n_idx)), axis=-1, keepdims=True)
        pick = (idx == first) & (m > -jnp.inf)
        sel = jnp.where(pick, 1.0, sel)
        g = jnp.where(pick, -jnp.inf, g)
    return sel


def _mm_kernel(x_ref, w_ref, o_ref, xb_ref):
    @pl.when(pl.program_id(1) == 0)
    def _():
        xb_ref[...] = _bf(x_ref[...])
    o_ref[...] = jnp.dot(xb_ref[...], w_ref[...], preferred_element_type=F32)


def _matmul(x, w_bf, tm, tn):
    m, k = x.shape
    n = w_bf.shape[1]
    tm, tn = _tile(m, tm), min(tn, n)
    return pl.pallas_call(
        _mm_kernel, grid=(m // tm, n // tn),
        in_specs=[pl.BlockSpec((tm, k), lambda i, j: (i, 0)),
                  pl.BlockSpec((k, tn), lambda i, j: (0, j))],
        out_specs=pl.BlockSpec((tm, tn), lambda i, j: (i, j)),
        out_shape=jax.ShapeDtypeStruct((m, n), F32),
        scratch_shapes=[pltpu.VMEM((tm, k), BF16)],
        compiler_params=_cp(("parallel", "arbitrary")), name="matmul")(x, w_bf)


def _proj_ln2_kernel(a1_ref, r1_ref, a2_ref, r2_ref, w_ref, g_ref, b_ref, o_ref, *, n_first):
    def emit(a_ref, r_ref):
        y = jnp.dot(_bf(a_ref[...]), w_ref[...], preferred_element_type=F32)
        o_ref[...] = _ln(DN_ALPHA * r_ref[...] + y, g_ref[...], b_ref[...])

    @pl.when(pl.program_id(0) < n_first)
    def _():
        emit(a1_ref, r1_ref)

    @pl.when(pl.program_id(0) >= n_first)
    def _():
        emit(a2_ref, r2_ref)


def _proj_res_ln_pair(a1, res1, a2, res2, w_bf, g, b, tm):
    m1, k = a1.shape
    m2 = a2.shape[0]
    n1, n2 = m1 // tm, m2 // tm
    first = lambda i: (jnp.minimum(i, n1 - 1), 0)
    second = lambda i: (jnp.maximum(i - n1, 0), 0)
    fix = lambda i: (0, 0)
    return pl.pallas_call(
        functools.partial(_proj_ln2_kernel, n_first=n1), grid=(n1 + n2,),
        in_specs=[pl.BlockSpec((tm, k), first), pl.BlockSpec((tm, D_MODEL), first),
                  pl.BlockSpec((tm, k), second), pl.BlockSpec((tm, D_MODEL), second),
                  pl.BlockSpec((k, D_MODEL), fix), pl.BlockSpec((1, D_MODEL), fix),
                  pl.BlockSpec((1, D_MODEL), fix)],
        out_specs=pl.BlockSpec((tm, D_MODEL), lambda i: (i, 0)),
        out_shape=jax.ShapeDtypeStruct((m1 + m2, D_MODEL), F32),
        compiler_params=_cp(("arbitrary",)), name="proj_res_ln")(a1, res1, a2, res2, w_bf, g, b)


def _gdn_pre(c, hs, bd, eb, eg, alog, dtb):
    q = c[:, :512]
    k = c[:, 512:1024]
    v = c[:, 1024:]
    qn = q * lax.rsqrt(_dot_x(q * q, bd, 2) + RMS_EPS) * (B_DK ** -0.5)
    kn = k * lax.rsqrt(_dot_x(k * k, bd, 2) + RMS_EPS)
    beta = jax.nn.sigmoid(hs)
    g = -jnp.exp(alog) * _softplus(hs + dtb)
    return qn, kn, v, _dot_x(beta, eb, 3), _dot_x(g, eg, 3)


def _post_p_kernel(bqkv_ref, hs_ref, cuv_ref, cw_ref, bd_ref, eb_ref, eg_ref, alog_ref, dtb_ref,
                   sg_ref, sb_ref, ws_ref, bx_ref,
                   qn_ref, kn_ref, v_ref, beta_ref, g_ref, yc_ref, xe_ref, *, tm):
    @pl.when(pl.program_id(1) == 0)
    def _():
        xe_ref[0:8, :] = jnp.zeros((8, B_CONV_CH), F32)
    xin = bqkv_ref[...]
    xe_ref[8:8 + tm, :] = xin
    cw = cw_ref[...]
    acc = xe_ref[5:5 + tm, :] * cw[0:1]
    acc = acc + xe_ref[6:6 + tm, :] * cw[1:2]
    acc = acc + xe_ref[7:7 + tm, :] * cw[2:3]
    acc = acc + xin * cw[3:4]
    xe_ref[0:8, :] = xe_ref[tm:tm + 8, :]
    c = acc * jax.nn.sigmoid(acc)
    qn, kn, v, bx, gx = _gdn_pre(c, hs_ref[...], bd_ref[...], eb_ref[...], eg_ref[...],
                                 alog_ref[...], dtb_ref[...])
    qn_ref[...] = qn
    kn_ref[...] = kn
    v_ref[...] = v
    beta_ref[...] = bx
    g_ref[...] = gx
    u = jax.nn.gelu(cuv_ref[:, :C_WIDTH])
    vr = _bf(_ln(jax.nn.gelu(cuv_ref[:, C_WIDTH:]), sg_ref[...], sb_ref[...]))
    for ch in range(tm // C_CHUNK):
        rs = slice(ch * C_CHUNK, (ch + 1) * C_CHUNK)
        for gi in range(C_GROUPS):
            cs = slice(gi * LANES, (gi + 1) * LANES)
            mixed = jnp.dot(ws_ref[gi], vr[rs, cs], preferred_element_type=F32) + bx_ref[:, cs]
            yc_ref[rs, cs] = u[rs, cs] * mixed


def _post_prompt(hpa, hps, hpc, consts, sgu, bsz, seq, tm=256):
    n = bsz * seq
    nt = seq // tm
    row = lambda b, t: (b * nt + t, 0)
    fix2 = lambda b, t: (0, 0)
    fix3 = lambda b, t: (0, 0, 0)
    cw, bd, eb, eg, alog, dtb = consts
    sg, sb, ws, bx = sgu
    o512 = jax.ShapeDtypeStruct((n, 512), F32)
    return pl.pallas_call(
        functools.partial(_post_p_kernel, tm=tm), grid=(bsz, nt),
        in_specs=[pl.BlockSpec((tm, B_CONV_CH), lambda b, t: (b * nt + t, 1)),
                  pl.BlockSpec((tm, LANES), row),
                  pl.BlockSpec((tm, 2 * C_WIDTH), row),
                  pl.BlockSpec((CONV_W, B_CONV_CH), fix2),
                  pl.BlockSpec((512, 512), fix2), pl.BlockSpec((LANES, 512), fix2),
                  pl.BlockSpec((LANES, 512), fix2), pl.BlockSpec((1, LANES), fix2),
                  pl.BlockSpec((1, LANES), fix2),
                  pl.BlockSpec((1, C_WIDTH), fix2), pl.BlockSpec((1, C_WIDTH), fix2),
                  pl.BlockSpec((C_GROUPS, C_CHUNK, C_CHUNK), fix3),
                  pl.BlockSpec((C_CHUNK, C_WIDTH), fix2)],
        out_specs=[pl.BlockSpec((tm, 512), row)] * 6,
        out_shape=[o512] * 6,
        scratch_shapes=[pltpu.VMEM((tm + 8, B_CONV_CH), F32)],
        compiler_params=_cp(("parallel", "arbitrary")), name="post_prompt",
    )(hpa, hps, hpc, cw, bd, eb, eg, alog, dtb, sg, sb, ws, bx)


def _post_s_kernel(hpa_ref, hps_ref, hpc_ref, prev_ref, cw_ref, bd_ref, eb_ref, eg_ref, alog_ref,
                   dtb_ref, sg_ref, sb_ref, wr_ref, br_ref,
                   qn_ref, kn_ref, v_ref, beta_ref, g_ref, yc_ref, vr_ref, *, ts):
    cw = cw_ref[...]
    xe = [prev_ref[:, j * B_CONV_CH:(j + 1) * B_CONV_CH] for j in range(CONV_W - 1)]
    xe += [hpa_ref[:, p * W_A + 1536:p * W_A + 3072] for p in range(ts)]
    vrows = []
    for p in range(ts):
        acc = xe[p] * cw[0:1]
        for j in range(1, CONV_W):
            acc = acc + xe[p + j] * cw[j:j + 1]
        c = acc * jax.nn.sigmoid(acc)
        hs = hps_ref[:, p * LANES:(p + 1) * LANES]
        qn, kn, v, bx, gx = _gdn_pre(c, hs, bd_ref[...], eb_ref[...], eg_ref[...],
                                     alog_ref[...], dtb_ref[...])
        ps = slice(p * 512, (p + 1) * 512)
        qn_ref[:, ps] = qn
        kn_ref[:, ps] = kn
        v_ref[:, ps] = v
        beta_ref[:, ps] = bx
        g_ref[:, ps] = gx
        vr = _ln(jax.nn.gelu(hpc_ref[:, p * W_C + C_WIDTH:p * W_C + 2 * C_WIDTH]),
                 sg_ref[...], sb_ref[...])
        vr_ref[:, ps] = vr
        vrows.append(vr)
        mixed = br_ref[p:p + 1, :]
        for j in range(p + 1):
            mixed = mixed + wr_ref[p * ts + j:p * ts + j + 1, :] * vrows[j]
        yc_ref[:, ps] = jax.nn.gelu(hpc_ref[:, p * W_C:p * W_C + C_WIDTH]) * mixed


def _post_sample(hpa, hps, hpc, prev, consts, sgu, db, ts):
    cw, bd, eb, eg, alog, dtb = consts
    sg, sb, wr, br = sgu
    args = (hpa.reshape(db, ts * W_A), hps.reshape(db, ts * LANES), hpc.reshape(db, ts * W_C),
            prev.reshape(db, (CONV_W - 1) * B_CONV_CH), cw, bd, eb, eg, alog, dtb, sg, sb, wr, br)
    o = jax.ShapeDtypeStruct((db, ts * 512), F32)
    return pl.pallas_call(
        functools.partial(_post_s_kernel, ts=ts),
        out_shape=[o] * 7, compiler_params=_cp(None), name="post_sample")(*args)


def _gdn_kernel(q_ref, k_ref, v_ref, bx_ref, gx_ref, s0_ref, o_ref, sf_ref, s_scr,
                *, c_len, bb, npc, n_dbl):
    ci = pl.program_id(1)

    @pl.when(ci == 0)
    def _():
        s_scr[...] = s0_ref[...]

    c2 = 2 * c_len
    ri = _iota((c2, c2), 0)
    cj = _iota((c2, c2), 1)
    same = (ri >= c_len) == (cj >= c_len)
    low = same & (ri >= cj)
    strict = same & (ri > cj)
    eye2 = jnp.where(ri == cj, 1.0, 0.0)
    tril_c = jnp.where(_iota((c_len, c_len), 0) >= _iota((c_len, c_len), 1), 1.0, 0.0).astype(BF16)
    eye_l = jnp.where(_iota((LANES, LANES), 0) == _iota((LANES, LANES), 1), 1.0, 0.0).astype(BF16)
    hsel = jnp.where(_iota((B_HEADS, B_WIDTH), 1) == _iota((B_HEADS, B_WIDTH), 0) * B_DK,
                     1.0, 0.0).astype(BF16)
    place = (_iota((c2, LANES), 1) < B_DK) == (_iota((c2, LANES), 0) < c_len)
    top_rows = _iota((c2, 1), 0) < c_len
    left_cols = _iota((1, c2), 1) < c_len

    chains = [(i, n, p) for i in range(bb) for n in range(npc) for p in range(HEAD_PAIRS)]
    op, decay, glast = {}, {}, {}
    for i in range(bb):
        for n in range(npc):
            rs = slice(n * c_len, (n + 1) * c_len)
            q, k, v = q_ref[i, rs, :], k_ref[i, rs, :], v_ref[i, rs, :]
            bx, gx = bx_ref[i, rs, :], gx_ref[i, rs, :]
            gcb = _xdot(tril_c, gx, 3)
            g2 = jnp.concatenate([gcb, gcb], axis=0)
            gct = _xdot(hsel, g2, 3, nt=True)
            eg = jnp.exp(gcb)
            g_last = gcb[c_len - 1:c_len, :]
            gl = jnp.exp(g_last)
            kb = k * bx
            wide = (k, kb, kb * eg, v * bx, q, q * eg, k * jnp.exp(g_last - gcb))
            for p in range(HEAD_PAIRS):
                ls = slice(p * LANES, (p + 1) * LANES)
                op[(i, n, p)] = [
                    _bf(jnp.where(place, jnp.concatenate([x[:, ls], x[:, ls]], axis=0), 0.0))
                    for x in wide]
                gcol = jnp.where(top_rows, g2[:, p * LANES:p * LANES + 1],
                                 g2[:, p * LANES + B_DK:p * LANES + B_DK + 1])
                grow = jnp.where(left_cols, gct[2 * p:2 * p + 1, :], gct[2 * p + 1:2 * p + 2, :])
                decay[(i, n, p)] = jnp.where(low, jnp.exp(jnp.where(low, gcol - grow, 0.0)), 0.0)
                glast[(i, n, p)] = gl[:, ls]
    mm = lambda x, y: jnp.dot(x, y, preferred_element_type=F32)
    mm_nt = lambda x, y: lax.dot_general(x, y, _NT, preferred_element_type=F32)
    kk = {c: mm_nt(op[c][1], op[c][0]) for c in chains}
    qk = {c: mm_nt(op[c][4], op[c][0]) for c in chains}
    kd_t = {c: _bf(mm_nt(eye_l, op[c][6])) for c in chains}
    a_mat = {c: jnp.where(strict, kk[c] * decay[c], 0.0) for c in chains}
    qk = {c: _bf(qk[c] * decay[c]) for c in chains}
    t_mat = {c: eye2 - a_mat[c] for c in chains}
    pw = a_mat
    for _ in range(n_dbl):
        pwb = {c: _bf(pw[c]) for c in chains}
        pw = {c: mm(pwb[c], pwb[c]) for c in chains}
        tp = {c: mm(_bf(t_mat[c]), _bf(pw[c])) for c in chains}
        t_mat = {c: t_mat[c] + tp[c] for c in chains}
    tb = {c: _bf(t_mat[c]) for c in chains}
    u = {c: mm(tb[c], op[c][3]) for c in chains}
    w = {c: _bf(mm(tb[c], op[c][2])) for c in chains}

    pairs = [(i, p) for i in range(bb) for p in range(HEAD_PAIRS)]
    state = {ip: s_scr[ip[0], ip[1]] for ip in pairs}
    for n in range(npc):
        cn = {ip: (ip[0], n, ip[1]) for ip in pairs}
        sb = {ip: _bf(state[ip]) for ip in pairs}
        ws = {ip: mm(w[cn[ip]], sb[ip]) for ip in pairs}
        qs_ = {ip: mm(op[cn[ip]][5], sb[ip]) for ip in pairs}
        vb = {ip: _bf(u[cn[ip]] - ws[ip]) for ip in pairs}
        ov = {ip: mm(qk[cn[ip]], vb[ip]) for ip in pairs}
        sv = {ip: mm(kd_t[cn[ip]], vb[ip]) for ip in pairs}
        for ip in pairs:
            o = qs_[ip] + ov[ip]
            o_ref[ip[0], n * c_len:(n + 1) * c_len, ip[1] * LANES:(ip[1] + 1) * LANES] = (
                o[:c_len] + o[c_len:])
            state[ip] = state[ip] * glast[cn[ip]] + sv[ip]
    for ip in pairs:
        s_scr[ip[0], ip[1]] = state[ip]

    @pl.when(ci == pl.num_programs(1) - 1)
    def _():
        sf_ref[...] = s_scr[...]


def _gdn(qn, kn, v, bx, gx, s0_bd, c_len, bb, npc):
    bsz, t, _ = qn.shape
    n_dbl = max(c_len.bit_length() - 2, 0)
    tile = npc * c_len
    row = pl.BlockSpec((bb, tile, 512), lambda b, c: (b, c, 0))
    st = pl.BlockSpec((bb, HEAD_PAIRS, LANES, LANES), lambda b, c: (b, 0, 0, 0))
    return pl.pallas_call(
        functools.partial(_gdn_kernel, c_len=c_len, bb=bb, npc=npc, n_dbl=n_dbl),
        grid=(bsz // bb, t // tile),
        in_specs=[row] * 5 + [st], out_specs=[row, st],
        out_shape=[jax.ShapeDtypeStruct((bsz, t, 512), F32),
                   jax.ShapeDtypeStruct((bsz, HEAD_PAIRS, LANES, LANES), F32)],
        scratch_shapes=[pltpu.VMEM((bb, HEAD_PAIRS, LANES, LANES), F32)],
        compiler_params=_cp(("parallel", "arbitrary")), name="gdn",
    )(qn, kn, v, bx, gx, s0_bd)


def _state_to_pairs(s):
    b = s.shape[0]
    s5 = s.reshape(b, HEAD_PAIRS, 2, B_DK, B_DK)
    z = jnp.zeros_like(s5[:, :, 0])
    top = jnp.concatenate([s5[:, :, 0], z], axis=-1)
    bot = jnp.concatenate([z, s5[:, :, 1]], axis=-1)
    return jnp.concatenate([top, bot], axis=-2)


def _state_from_pairs(sp):
    b = sp.shape[0]
    return jnp.stack([sp[:, :, :B_DK, :B_DK], sp[:, :, B_DK:, B_DK:]], axis=2).reshape(
        b, B_HEADS, B_DK, B_DK)


def _moba_p_kernel(q_ref, k_ref, v_ref, o_ref, km_ref, ka_ref, vb_ref, *, nb, n_sel, grp):
    own = pl.program_id(2)
    tq = MOBA_BLOCK
    heads = range(2)

    @pl.when(own == 0)
    def _():
        km_ref[...] = jnp.zeros(km_ref.shape, F32)
        lane = _iota((MOBA_BLOCK, LANES), 1)
        for n in range(nb):
            rows = slice(n * MOBA_BLOCK, (n + 1) * MOBA_BLOCK)
            kf = k_ref[0, rows, :]
            km_ref[n:n + 1, :] = jnp.mean(kf, axis=0, keepdims=True)
            vb_ref[rows, :] = _bf(v_ref[0, rows, :])
            ka_ref[rows, :LANES] = _bf(kf)
            ka_ref[rows, LANES:] = jnp.where(lane == n, 1.0, 0.0).astype(BF16)

    q2 = q_ref[0]
    lane = _iota((tq, LANES), 1)
    lane_f = lane.astype(F32)
    first_head = lane < A_HEAD_DIM
    scale = A_HEAD_DIM ** -0.5
    km = km_ref[...]
    qhb, qaug = [], []
    for h in heads:
        qh = jnp.where(first_head == (h == 0), q2, 0.0)
        g = _dot_nt_hl(qh, km)
        g = jnp.where(lane < own, g, -jnp.inf)
        sel = _top_mask(g, lane_f, LANES, n_sel)
        bias = jnp.where((lane < nb) & (sel == 0.0), NEG, 0.0)
        qhb.append(_bf(qh * scale))
        qaug.append(jnp.concatenate([qhb[h], _bf(bias)], axis=1))

    start = pl.multiple_of(own * MOBA_BLOCK, MOBA_BLOCK)
    k_own = _bf(k_ref[0, pl.ds(start, MOBA_BLOCK), :])
    v_own = vb_ref[pl.ds(start, MOBA_BLOCK), :]
    causal = _iota((tq, MOBA_BLOCK), 1) <= _iota((tq, MOBA_BLOCK), 0)
    s_own = [jnp.where(causal, lax.dot_general(qhb[h], k_own, _NT, preferred_element_type=F32), NEG)
             for h in heads]
    span = grp * MOBA_BLOCK
    n_grp = (own + grp - 1) // grp

    def group_scores(gi):
        st = pl.multiple_of(gi * span, span)
        kg = ka_ref[pl.ds(st, span), :]
        return st, [lax.dot_general(qaug[h], kg, _NT, preferred_element_type=F32) for h in heads]

    def body(gi, carry):
        st, ss = group_scores(gi)
        vg = vb_ref[pl.ds(st, span), :]
        m_new = [jnp.maximum(carry[3 * h], jnp.max(ss[h], axis=-1, keepdims=True)) for h in heads]
        ps = [jnp.exp(ss[h] - m_new[h]) for h in heads]
        pv = [jnp.dot(_bf(ps[h]), vg, preferred_element_type=F32) for h in heads]
        out = []
        for h in heads:
            alpha = jnp.exp(carry[3 * h] - m_new[h])
            l = carry[3 * h + 1] * alpha + jnp.sum(ps[h], axis=-1, keepdims=True)
            out += [m_new[h], l, carry[3 * h + 2] * alpha + pv[h]]
        return tuple(out)

    init = []
    for h in heads:
        m = jnp.max(s_own[h], axis=-1, keepdims=True)
        p = jnp.exp(s_own[h] - m)
        init += [m, jnp.sum(p, axis=-1, keepdims=True),
                 jnp.dot(_bf(p), v_own, preferred_element_type=F32)]
    _, l0, a0, _, l1, a1 = lax.fori_loop(0, n_grp, body, tuple(init))
    o_ref[0] = jnp.where(first_head, a0 / l0, a1 / l1)


def _moba_prompt(hpa3, bsz, seq):
    nb = seq // MOBA_BLOCK
    assert nb <= LANES, "one block-indicator lane per key block"
    grp = KV_GROUP if nb % KV_GROUP == 0 else 1
    npair = A_WIDTH // LANES
    return pl.pallas_call(
        functools.partial(_moba_p_kernel, nb=nb, n_sel=min(MOBA_TOPK, nb - 1), grp=grp),
        grid=(bsz, npair, nb),
        in_specs=[pl.BlockSpec((1, MOBA_BLOCK, LANES), lambda b, p, i: (b, i, p)),
                  pl.BlockSpec((1, seq, LANES), lambda b, p, i: (b, 0, npair + p)),
                  pl.BlockSpec((1, seq, LANES), lambda b, p, i: (b, 0, 2 * npair + p))],
        out_specs=pl.BlockSpec((1, MOBA_BLOCK, LANES), lambda b, p, i: (b, i, p)),
        out_shape=jax.ShapeDtypeStruct((bsz, seq, A_WIDTH), F32),
        scratch_shapes=[pltpu.VMEM((LANES, LANES), F32), pltpu.VMEM((seq, 2 * LANES), BF16),
                        pltpu.VMEM((seq, LANES), BF16)],
        compiler_params=_cp(("parallel", "parallel", "arbitrary")), name="moba_prompt",
    )(hpa3, hpa3, hpa3)


def _moba_s_kernel(pt_ref, q_ref, kn_ref, vn_ref, *refs, n_pages, ts, page):
    kt_pages = refs[:n_pages]
    vt_pages = refs[n_pages:2 * n_pages]
    o_ref, s_ref = refs[2 * n_pages:]
    ppb = MOBA_BLOCK // page
    nb = n_pages // ppb
    rows = A_HEADS * 8
    q8 = q_ref[0]
    lane_head = _iota((8, A_WIDTH), 1) // A_HEAD_DIM
    qbd = jnp.concatenate([jnp.where(lane_head == h, q8, 0.0) for h in range(A_HEADS)], axis=0)
    qbd_b = _bf(qbd)
    scale = A_HEAD_DIM ** -0.5
    col = _iota((A_WIDTH, LANES), 1)
    km = jnp.zeros((A_WIDTH, LANES), F32)
    for n in range(nb):
        tot = None
        for pp in range(ppb):
            sm = jnp.sum(kt_pages[n * ppb + pp][0, 0], axis=1, keepdims=True)
            tot = sm if tot is None else tot + sm
        km = jnp.where(col == n, tot * (1.0 / MOBA_BLOCK), km)
    blk = _iota((rows, LANES), 1)
    g = _dot_hl(qbd, km)
    g = jnp.where(blk < nb, g, -jnp.inf)
    sel = _top_mask(g, blk.astype(F32), LANES, min(MOBA_TOPK, nb))
    s_own = lax.dot_general(qbd_b, _bf(kn_ref[0]), _NT, preferred_element_type=F32) * scale
    kr = _iota((rows, 8), 1)
    qr = _iota((rows, 8), 0) % 8
    s_own = jnp.where((kr <= qr) & (kr < ts), s_own, NEG)
    m_tile = jnp.full((rows, page), NEG, F32)
    for pg in range(n_pages):
        s = jnp.dot(qbd_b, _bf(kt_pages[pg][0, 0]), preferred_element_type=F32) * scale
        s = jnp.where(sel[:, pg // ppb:pg // ppb + 1] > 0.0, s, NEG)
        s_ref[:, pg * page:(pg + 1) * page] = s
        m_tile = jnp.maximum(m_tile, s)
    m = jnp.maximum(jnp.max(s_own, axis=-1, keepdims=True), jnp.max(m_tile, axis=-1, keepdims=True))
    p_own = jnp.exp(s_own - m)
    l_tile = jnp.zeros((rows, page), F32)
    for pg in range(n_pages):
        p = jnp.exp(s_ref[:, pg * page:(pg + 1) * page] - m)
        s_ref[:, pg * page:(pg + 1) * page] = p
        l_tile = l_tile + p
    l = jnp.sum(p_own, axis=-1, keepdims=True) + jnp.sum(l_tile, axis=-1, keepdims=True)
    inv = 1.0 / l
    o = jnp.dot(_bf(p_own * inv), _bf(vn_ref[0]), preferred_element_type=F32)
    for pg in range(n_pages):
        p = s_ref[:, pg * page:(pg + 1) * page] * inv
        o = o + lax.dot_general(_bf(p), _bf(vt_pages[pg][0, 0]), _NT, preferred_element_type=F32)
    out = jnp.zeros((8, A_WIDTH), F32)
    for h in range(A_HEADS):
        out = out + jnp.where(lane_head == h, o[h * 8:(h + 1) * 8, :], 0.0)
    o_ref[0] = out


def _moba_sample(q8, kn8, vn8, cache_kt, cache_vt, pt_flat, layer, db, n_pages, ts):
    page = cache_kt.shape[3]
    tok = pl.BlockSpec((1, 8, A_WIDTH), lambda b, pt: (b, 0, 0))

    def page_spec(p):
        return pl.BlockSpec((1, 1, A_WIDTH, page), lambda b, pt: (pt[b * n_pages + p], layer, 0, 0))

    specs = [page_spec(p) for p in range(n_pages)]
    grid_spec = pltpu.PrefetchScalarGridSpec(
        num_scalar_prefetch=1, grid=(db,),
        in_specs=[tok, tok, tok] + specs + specs,
        out_specs=tok,
        scratch_shapes=[pltpu.VMEM((A_HEADS * 8, n_pages * page), F32)])
    return pl.pallas_call(
        functools.partial(_moba_s_kernel, n_pages=n_pages, ts=ts, page=page),
        grid_spec=grid_spec, out_shape=jax.ShapeDtypeStruct((db, 8, A_WIDTH), F32),
        compiler_params=_cp(("parallel",)), name="moba_sample",
    )(pt_flat, q8, kn8, vn8, *([cache_kt] * n_pages), *([cache_vt] * n_pages))


def _merge_kernel(ya_ref, yb_ref, z_ref, yc_ref, g0_ref, g1_ref, g2_ref, x_ref, bd_ref, ng_ref,
                  woa_ref, wob_ref, woc_ref, wmo_ref, lg_ref, lb_ref, o_ref):
    f = lambda y, w: jnp.dot(_bf(y), w[...], preferred_element_type=F32)
    yb = yb_ref[...]
    z = z_ref[...]
    ms = _dot_x(yb * yb, bd_ref[...], 2) * (1.0 / B_DK)
    yb = yb * lax.rsqrt(ms + RMS_EPS) * ng_ref[...] * (z * jax.nn.sigmoid(z))
    merged = jax.nn.sigmoid(g0_ref[...]) * f(ya_ref[...], woa_ref)
    merged = merged + jax.nn.sigmoid(g1_ref[...]) * f(yb, wob_ref)
    merged = merged + jax.nn.sigmoid(g2_ref[...]) * f(yc_ref[...], woc_ref)
    y = jnp.dot(_bf(merged), wmo_ref[...], preferred_element_type=F32)
    o_ref[...] = _ln(DN_ALPHA * x_ref[...] + y, lg_ref[...], lb_ref[...])


def _merge(ya, yb, hpa, yc, hpc, x, bd, ng, woa, wob, woc, wmo, lg, lb, tm=256):
    m = x.shape[0]
    tm = _tile(m, tm)
    row = lambda i: (i, 0)
    fix = lambda i: (0, 0)
    y512 = pl.BlockSpec((tm, 512), row)
    gate = lambda j: pl.BlockSpec((tm, D_MODEL), lambda i: (i, j))
    wsm = pl.BlockSpec((512, D_MODEL), fix)
    vec = pl.BlockSpec((1, D_MODEL), fix)
    return pl.pallas_call(
        _merge_kernel, grid=(m // tm,),
        in_specs=[y512, y512, pl.BlockSpec((tm, 512), lambda i: (i, OFF_BZ // 512)), y512,
                  gate(1), gate(2), gate(3), pl.BlockSpec((tm, D_MODEL), row),
                  pl.BlockSpec((512, 512), fix), pl.BlockSpec((1, 512), fix),
                  wsm, wsm, wsm, pl.BlockSpec((D_MODEL, D_MODEL), fix), vec, vec],
        out_specs=pl.BlockSpec((tm, D_MODEL), row),
        out_shape=jax.ShapeDtypeStruct((m, D_MODEL), F32),
        compiler_params=_cp(("parallel",)), name="merge",
    )(ya, yb, hpa, yc, hpc, hpc, hpc, x, bd, ng, woa, wob, woc, wmo, lg, lb)


def _attend(q, k, v):
    s = _dot_nt(q, k) * (X_HEAD_DIM ** -0.5)
    p = jnp.exp(s - jnp.max(s, axis=-1, keepdims=True))
    p = p / jnp.sum(p, axis=-1, keepdims=True)
    return _dot(p, v)


def _xattn_kernel(q_ref, k_ref, v_ref, o_ref):
    for h in range(X_HEADS):
        cs = slice(h * X_HEAD_DIM, (h + 1) * X_HEAD_DIM)
        o_ref[0, :, cs] = _attend(q_ref[0, :, cs], k_ref[0, :, cs], v_ref[0, :, cs])


def _xattn(q3, k3, v3, tm):
    g, r, _ = q3.shape
    tm = _tile(r, tm)
    mem = k3.shape[1]
    qs = pl.BlockSpec((1, tm, D_MODEL), lambda i, t: (i, t, 0))
    ms = pl.BlockSpec((1, mem, D_MODEL), lambda i, t: (i, 0, 0))
    return pl.pallas_call(
        _xattn_kernel, grid=(g, r // tm), in_specs=[qs, ms, ms], out_specs=qs,
        out_shape=jax.ShapeDtypeStruct(q3.shape, F32),
        compiler_params=_cp(("parallel", "parallel")), name="xattn")(q3, k3, v3)


def _xattn_cache_kernel(q_ref, k_hbm, v_hbm, o_ref, kbuf, vbuf, sem, *, nseq, layer):
    units = [(g, h) for g in range(nseq) for h in range(X_HEADS)]
    base = pl.program_id(0) * nseq

    def head_copies(g, h):
        return (pltpu.make_async_copy(k_hbm.at[base + g, layer, :, h, :], kbuf.at[g, h], sem.at[0]),
                pltpu.make_async_copy(v_hbm.at[base + g, layer, :, h, :], vbuf.at[g, h], sem.at[1]))

    for u in units:
        for c in head_copies(*u):
            c.start()
    for u in units:
        for c in head_copies(*u):
            c.wait()
    cs = lambda h: slice(h * X_HEAD_DIM, (h + 1) * X_HEAD_DIM)
    scale = X_HEAD_DIM ** -0.5
    s = {u: _dot_nt(q_ref[u[0], :, cs(u[1])] * scale, kbuf[u[0], u[1]]) for u in units}
    p = {u: jnp.exp(s[u] - jnp.max(s[u], axis=-1, keepdims=True)) for u in units}
    p = {u: p[u] / jnp.sum(p[u], axis=-1, keepdims=True) for u in units}
    o = {u: _dot(p[u], vbuf[u[0], u[1]]) for u in units}
    for g, h in units:
        o_ref[g, :, cs(h)] = o[(g, h)]


def _xattn_cache(q3, cache_k, cache_v, layer, nseq=4):
    g = q3.shape[0]
    mem = cache_k.shape[2]
    qs = pl.BlockSpec((nseq, 8, D_MODEL), lambda i: (i, 0, 0))
    anyspec = pl.BlockSpec(memory_space=pl.ANY)
    buf = pltpu.VMEM((nseq, X_HEADS, mem, X_HEAD_DIM), F32)
    return pl.pallas_call(
        functools.partial(_xattn_cache_kernel, nseq=nseq, layer=layer), grid=(g // nseq,),
        in_specs=[qs, anyspec, anyspec], out_specs=qs,
        out_shape=jax.ShapeDtypeStruct(q3.shape, F32),
        scratch_shapes=[buf, buf, pltpu.SemaphoreType.DMA((2,))],
        compiler_params=_cp(("arbitrary",)), name="xattn_cache")(q3, cache_k, cache_v)


def _router_kernel(x_ref, wr_ref, br_ref, tri_ref, idx_ref, gate_ref, rank_ref, cnt_ref, carry_ref):
    @pl.when(pl.program_id(0) == 0)
    def _():
        carry_ref[...] = jnp.zeros(carry_ref.shape, F32)

    logits = _dot_hl(x_ref[...], wr_ref[...]) + br_ref[...]
    tm = logits.shape[0]
    lane = _iota((tm, LANES), 1)
    lane_f = lane.astype(F32)
    lg = logits
    onehots, tops, firsts = [], [], []
    for _ in range(TOP_K):
        m = jnp.max(lg, axis=-1, keepdims=True)
        first = jnp.min(jnp.where(lg == m, lane_f, float(LANES)), axis=-1, keepdims=True)
        pick = lane_f == first
        lg = jnp.where(pick, -jnp.inf, lg)
        onehots.append(jnp.where(pick, 1.0, 0.0))
        tops.append(m)
        firsts.append(first)
    es = [jnp.exp(t - tops[0]) for t in tops]
    den = es[0] + es[1] + es[2] + es[3]
    all_hot = onehots[0] + onehots[1] + onehots[2] + onehots[3]
    before = jnp.dot(tri_ref[...], _bf(all_hot), preferred_element_type=F32) + carry_ref[...]
    idx_o = jnp.zeros((tm, LANES), F32)
    gate_o = jnp.zeros((tm, LANES), F32)
    rank_o = jnp.zeros((tm, LANES), F32)
    for k in range(TOP_K):
        rk = jnp.sum(onehots[k] * before, axis=-1, keepdims=True)
        idx_o = jnp.where(lane == k, firsts[k], idx_o)
        gate_o = jnp.where(lane == k, es[k] / den, gate_o)
        rank_o = jnp.where(lane == k, rk, rank_o)
    idx_ref[...] = idx_o.astype(jnp.int32)
    gate_ref[...] = gate_o
    rank_ref[...] = rank_o.astype(jnp.int32)
    carry_ref[...] = carry_ref[...] + jnp.sum(all_hot, axis=0, keepdims=True)
    cnt_ref[...] = carry_ref[...].astype(jnp.int32)


def _router(x, wr_pad, br_pad, tm=512):
    n = x.shape[0]
    tm = _tile(n, tm)
    tri = jnp.tril(jnp.ones((tm, tm), F32), -1).astype(BF16)
    row = lambda i: (i, 0)
    fix = lambda i: (0, 0)
    o = pl.BlockSpec((tm, LANES), row)
    return pl.pallas_call(
        _router_kernel, grid=(n // tm,),
        in_specs=[pl.BlockSpec((tm, D_MODEL), row), pl.BlockSpec((D_MODEL, LANES), fix),
                  pl.BlockSpec((1, LANES), fix), pl.BlockSpec((tm, tm), fix)],
        out_specs=[o, o, o, pl.BlockSpec((1, LANES), fix)],
        out_shape=[jax.ShapeDtypeStruct((n, LANES), jnp.int32), jax.ShapeDtypeStruct((n, LANES), F32),
                   jax.ShapeDtypeStruct((n, LANES), jnp.int32),
                   jax.ShapeDtypeStruct((1, LANES), jnp.int32)],
        scratch_shapes=[pltpu.VMEM((1, LANES), F32)],
        compiler_params=_cp(("arbitrary",)), name="router")(x, wr_pad, br_pad, tri)


def _dispatch_kernel(dest_ref, x_ref, xs_in_ref, xs_ref, sem, *, tm):
    del xs_in_ref

    def row_copy(t, k):
        d = dest_ref[t * TOP_K + k]
        return pltpu.make_async_copy(x_ref.at[pl.ds(t, 1), :], xs_ref.at[pl.ds(d, 1), :], sem)

    def issue(t, c):
        for k in range(TOP_K):
            row_copy(t, k).start(priority=k % 2)
        return c

    def drain(t, c):
        for k in range(TOP_K):
            row_copy(t, k).wait()
        return c

    lax.fori_loop(0, tm, issue, 0)
    lax.fori_loop(0, tm, drain, 0)


def _dispatch(x, dest_flat, xs_zero, tm=256):
    n = x.shape[0]
    return pl.pallas_call(
        functools.partial(_dispatch_kernel, tm=tm), grid=(n // tm,),
        in_specs=[pl.BlockSpec((tm * TOP_K,), lambda i: (i,), memory_space=pltpu.SMEM),
                  pl.BlockSpec((tm, D_MODEL), lambda i: (i, 0)),
                  pl.BlockSpec(memory_space=pl.ANY)],
        out_specs=pl.BlockSpec(memory_space=pl.ANY),
        out_shape=jax.ShapeDtypeStruct(xs_zero.shape, F32),
        scratch_shapes=[pltpu.SemaphoreType.DMA(())],
        input_output_aliases={2: 0},
        compiler_params=_cp(("arbitrary",)), name="moe_dispatch")(dest_flat, x, xs_zero)


def _ffn_kernel(be_ref, nu_ref, x_ref, wgu_ref, bgu_ref, wd_ref, bd_ref, o_ref, wgu_b, wd_b):
    i = pl.program_id(0)
    prev = be_ref[jnp.maximum(i - 1, 0)]
    used = i < nu_ref[0]

    @pl.when(used & ((i == 0) | (be_ref[i] != prev)))
    def _():
        wgu_b[...] = _bf(wgu_ref[0, 0])
        wd_b[...] = _bf(wd_ref[0, 0])

    @pl.when(used)
    def _():
        hgu = jnp.dot(_bf(x_ref[...]), wgu_b[...], preferred_element_type=F32) + bgu_ref[0, 0]
        gate = jnp.minimum(hgu[:, :D_FF], SWIGLU_LIMIT)
        up = jnp.clip(hgu[:, D_FF:], -SWIGLU_LIMIT, SWIGLU_LIMIT)
        act = (up + 1.0) * gate * jax.nn.sigmoid(gate * SWIGLU_ALPHA)
        o_ref[...] = jnp.dot(_bf(act), wd_b[...], preferred_element_type=F32) + bd_ref[0, 0]

    @pl.when(jnp.logical_not(used))
    def _():
        o_ref[...] = jnp.zeros(o_ref.shape, F32)


def _expert_ffn(xs, blk_e, n_used, w_gu, b_gu, w_d, b_d, layer):
    n_blocks = xs.shape[0] // MOE_BLK
    rowi = lambda i, be, nu: (jnp.minimum(i, nu[0] - 1), 0)
    rowo = lambda i, be, nu: (i, 0)
    exp4 = lambda i, be, nu: (layer, be[i], 0, 0)
    grid_spec = pltpu.PrefetchScalarGridSpec(
        num_scalar_prefetch=2, grid=(n_blocks,),
        in_specs=[pl.BlockSpec((MOE_BLK, D_MODEL), rowi),
                  pl.BlockSpec((1, 1, D_MODEL, 2 * D_FF), exp4), pl.BlockSpec((1, 1, 1, 2 * D_FF), exp4),
                  pl.BlockSpec((1, 1, D_FF, D_MODEL), exp4), pl.BlockSpec((1, 1, 1, D_MODEL), exp4)],
        out_specs=pl.BlockSpec((MOE_BLK, D_MODEL), rowo),
        scratch_shapes=[pltpu.VMEM((D_MODEL, 2 * D_FF), BF16), pltpu.VMEM((D_FF, D_MODEL), BF16)])
    return pl.pallas_call(
        _ffn_kernel, grid_spec=grid_spec, out_shape=jax.ShapeDtypeStruct(xs.shape, F32),
        compiler_params=_cp(("arbitrary",)), name="expert_ffn",
    )(blk_e, n_used, xs, w_gu, b_gu.reshape(DEPTH, N_EXPERTS, 1, -1), w_d,
      b_d.reshape(DEPTH, N_EXPERTS, 1, -1))


def _combine_kernel(dest_ref, dest_nx_ref, gate_ref, x_ref, lg_ref, lb_ref, yb_ref, op_ref, os_ref,
                    buf, sem, *, tm, n_first):
    i = pl.program_id(0)
    slot = i % 2

    def row_copy(dref, s, t, k):
        d = dref[t * TOP_K + k]
        return pltpu.make_async_copy(yb_ref.at[pl.ds(d, 1), :], buf.at[s, k, pl.ds(t, 1), :], sem.at[s])

    def issue(dref, s):
        def step(t, c):
            for k in range(TOP_K):
                row_copy(dref, s, t, k).start(priority=k % 2)
            return c
        lax.fori_loop(0, tm, step, 0)

    @pl.when(i == 0)
    def _():
        issue(dest_ref, 0)

    @pl.when(i + 1 < pl.num_programs(0))
    def _():
        issue(dest_nx_ref, 1 - slot)

    def drain(t, c):
        for k in range(TOP_K):
            row_copy(dest_ref, slot, t, k).wait()
        return c

    lax.fori_loop(0, tm, drain, 0)
    gates = gate_ref[...]
    y = buf[slot, 0] * gates[:, 0:1]
    for k in range(1, TOP_K):
        y = y + buf[slot, k] * gates[:, k:k + 1]
    out = _ln(DN_ALPHA * x_ref[...] + y, lg_ref[...], lb_ref[...])

    @pl.when(i < n_first)
    def _():
        op_ref[...] = out

    @pl.when(i >= n_first)
    def _():
        os_ref[...] = out


def _combine(yb, dest_flat, gates, x, lg, lb, n_first_rows, tm=256):
    n = x.shape[0]
    nt = n // tm
    n_first = n_first_rows // tm
    row = lambda i: (i, 0)
    fix = lambda i: (0, 0)
    return pl.pallas_call(
        functools.partial(_combine_kernel, tm=tm, n_first=n_first), grid=(nt,),
        in_specs=[pl.BlockSpec((tm * TOP_K,), lambda i: (i,), memory_space=pltpu.SMEM),
                  pl.BlockSpec((tm * TOP_K,), lambda i: (jnp.minimum(i + 1, nt - 1),),
                               memory_space=pltpu.SMEM),
                  pl.BlockSpec((tm, LANES), row), pl.BlockSpec((tm, D_MODEL), row),
                  pl.BlockSpec((1, D_MODEL), fix), pl.BlockSpec((1, D_MODEL), fix),
                  pl.BlockSpec(memory_space=pl.ANY)],
        out_specs=[pl.BlockSpec((tm, D_MODEL), lambda i: (jnp.minimum(i, n_first - 1), 0)),
                   pl.BlockSpec((tm, D_MODEL), lambda i: (jnp.maximum(i - n_first, 0), 0))],
        out_shape=[jax.ShapeDtypeStruct((n_first_rows, D_MODEL), F32),
                   jax.ShapeDtypeStruct((n - n_first_rows, D_MODEL), F32)],
        scratch_shapes=[pltpu.VMEM((2, TOP_K, tm, D_MODEL), F32), pltpu.SemaphoreType.DMA((2,))],
        compiler_params=_cp(("arbitrary",)), name="moe_combine",
    )(dest_flat, dest_flat, gates, x, lg, lb, yb)


def _moe(x, n_first_rows, wr_pad, br_pad, w_gu, b_gu, w_d, b_d, layer, lg, lb):
    n = x.shape[0]
    idx, gates, rank, cnt = _router(x, wr_pad, br_pad)
    counts = cnt[0, :N_EXPERTS]
    padded = (counts + MOE_BLK - 1) // MOE_BLK * MOE_BLK
    pad_end = jnp.cumsum(padded)
    pad_start = pad_end - padded
    e_sel = idx[:, :TOP_K]
    dest = (pad_start[e_sel] + rank[:, :TOP_K]).reshape(-1).astype(jnp.int32)
    n_blocks = -(-(n * TOP_K) // MOE_BLK) + N_EXPERTS
    blk_e = jnp.sum(pad_end[None, :] <= (jnp.arange(n_blocks) * MOE_BLK)[:, None], axis=1)
    blk_e = jnp.minimum(blk_e, N_EXPERTS - 1).astype(jnp.int32)
    n_used = (pad_end[-1:] // MOE_BLK).astype(jnp.int32)
    xs = _dispatch(x, dest, jnp.zeros((n_blocks * MOE_BLK, D_MODEL), F32))
    yb = _expert_ffn(xs, blk_e, n_used, w_gu, b_gu, w_d, b_d, layer)
    return _combine(yb, dest, gates, x, lg, lb, n_first_rows)


def _layer_consts(l, w_in, gdn_conv_w, gdn_a_log, gdn_dt_bias, sgu_norm_g, sgu_norm_b, sgu_w, sgu_b, ts):
    wi = w_in[l]
    w_a = _bf(wi[:, :W_A])
    w_s = _bf(jnp.pad(wi[:, OFF_BB:OFF_CU], ((0, 0), (0, LANES - 2 * B_HEADS))))
    w_c = _bf(wi[:, OFF_CU:])
    lane = jnp.arange(512) // B_DK
    head = jnp.arange(LANES)
    bd = (lane[:, None] == lane[None, :]).astype(BF16)
    eb = (head[:, None] == lane[None, :]).astype(BF16)
    eg = (head[:, None] == lane[None, :] + B_HEADS).astype(BF16)
    alog = jnp.pad(gdn_a_log[l], (B_HEADS, LANES - 2 * B_HEADS)).reshape(1, LANES)
    dtb = jnp.pad(gdn_dt_bias[l], (B_HEADS, LANES - 2 * B_HEADS)).reshape(1, LANES)
    consts = (gdn_conv_w[l], bd, eb, eg, alog, dtb)
    sg, sb = sgu_norm_g[l].reshape(1, -1), sgu_norm_b[l].reshape(1, -1)
    tri = jnp.tril(jnp.ones((C_CHUNK, C_CHUNK), F32))
    ws_p = _bf(sgu_w[l] * tri)
    bx_p = jnp.repeat(sgu_b[l].T, C_CHUNK, axis=1)
    wr_s = jnp.repeat(sgu_w[l][:, :ts, :ts].transpose(1, 2, 0).reshape(ts * ts, C_GROUPS), C_CHUNK, axis=1)
    br_s = jnp.repeat(sgu_b[l][:, :ts].T, C_CHUNK, axis=1)
    return w_a, w_s, w_c, consts, (sg, sb, ws_p, bx_p), (sg, sb, wr_s, br_s)


def kernel(x_prompt, x_sample, mem_prompt, cache_moba_k, cache_moba_v, page_table, state_gdn, state_gdn_conv, cache_mem_k, cache_mem_v, w_in, gdn_conv_w, gdn_a_log, gdn_dt_bias, gdn_norm_g, sgu_norm_g, sgu_norm_b, sgu_w, sgu_b, w_out_a, w_out_b, w_out_c, w_mix_out, w_xq, w_xk, w_xv, w_xo, w_router, b_router, w_gate_up, b_gate_up, w_down, b_down, ln_g, ln_b):
    bp, seq, _ = x_prompt.shape
    db, ts, _ = x_sample.shape
    n_pages = page_table.shape[1]
    n_pool, _, page, _, _ = cache_moba_k.shape
    mem_len = mem_prompt.shape[1]
    np_, ns = bp * seq, db * ts
    ckt = cache_moba_k.transpose(0, 1, 3, 4, 2).reshape(n_pool, DEPTH, A_WIDTH, page)
    cvt = cache_moba_v.transpose(0, 1, 3, 4, 2).reshape(n_pool, DEPTH, A_WIDTH, page)
    pt_flat = page_table.reshape(-1).astype(jnp.int32)
    mem2 = mem_prompt.reshape(bp * mem_len, D_MODEL)
    pad8 = lambda a: jnp.pad(a.reshape(db, ts, -1), ((0, 0), (0, 8 - ts), (0, 0)))
    gdn_npc = 2 if (seq // GDN_CHUNK) % 2 == 0 else 1

    yp = x_prompt.reshape(np_, D_MODEL)
    ys = x_sample.reshape(ns, D_MODEL)
    outs = {k: [] for k in ("kp", "vp", "ks", "vs", "sp", "ss", "cp", "cs", "mk", "mv", "vr")}
    for l in range(DEPTH):
        w_a, w_s, w_c, consts, sgu_p, sgu_s = _layer_consts(
            l, w_in, gdn_conv_w, gdn_a_log, gdn_dt_bias, sgu_norm_g, sgu_norm_b, sgu_w, sgu_b, ts)
        bd = consts[1]
        woa, wob, woc, wmo = _bf(w_out_a[l]), _bf(w_out_b[l]), _bf(w_out_c[l]), _bf(w_mix_out[l])
        wxq, wxo = _bf(w_xq[l]), _bf(w_xo[l])
        lg = [ln_g[l, i].reshape(1, -1) for i in range(3)]
        lb = [ln_b[l, i].reshape(1, -1) for i in range(3)]
        ng = jnp.tile(gdn_norm_g[l], B_HEADS).reshape(1, -1)
        wr_pad = jnp.pad(w_router[l], ((0, 0), (0, LANES - N_EXPERTS)))
        br_pad = jnp.pad(b_router[l], (0, LANES - N_EXPERTS), constant_values=NEG).reshape(1, LANES)

        mk = _matmul(mem2, _bf(w_xk[l]), 512, 512)
        mv = _matmul(mem2, _bf(w_xv[l]), 512, 512)
        hpa = _matmul(yp, w_a, 1024, 512)
        hps = _matmul(yp, w_s, 1024, LANES)
        hpc = _matmul(yp, w_c, 1024, 512)
        qn, kn, vv, bx, gx, yc = _post_prompt(hpa, hps, hpc, consts, sgu_p, bp, seq)
        hpa3 = hpa.reshape(bp, seq, W_A)
        ya = _moba_prompt(hpa3, bp, seq).reshape(np_, A_WIDTH)
        r3 = lambda a: a.reshape(bp, seq, 512)
        s0 = jnp.zeros((bp, HEAD_PAIRS, LANES, LANES), F32)
        yb, sp = _gdn(r3(qn), r3(kn), r3(vv), r3(bx), r3(gx), s0, GDN_CHUNK, bp, gdn_npc)
        x1 = _merge(ya, yb.reshape(np_, 512), hpa, yc, hpc, yp, bd, ng, woa, wob, woc, wmo, lg[0], lb[0])
        qx = _matmul(x1, wxq, 1024, 512)
        ox = _xattn(qx.reshape(bp, seq, D_MODEL), mk.reshape(bp, mem_len, D_MODEL),
                    mv.reshape(bp, mem_len, D_MODEL), 512).reshape(np_, D_MODEL)
        outs["kp"].append(hpa3[:, :, 512:1024].reshape(bp, seq, A_HEADS, A_HEAD_DIM))
        outs["vp"].append(hpa3[:, :, 1024:1536].reshape(bp, seq, A_HEADS, A_HEAD_DIM))
        outs["cp"].append(hpa3[:, seq - (CONV_W - 1):, 1536:3072])
        outs["sp"].append(_state_from_pairs(sp))
        outs["mk"].append(mk.reshape(bp, mem_len, X_HEADS, X_HEAD_DIM))
        outs["mv"].append(mv.reshape(bp, mem_len, X_HEADS, X_HEAD_DIM))

        hpa_s = _matmul(ys, w_a, 512, 512)
        hps_s = _matmul(ys, w_s, 512, LANES)
        hpc_s = _matmul(ys, w_c, 512, 512)
        qn_s, kn_s, v_s, bx_s, gx_s, yc_s, vr_s = _post_sample(
            hpa_s, hps_s, hpc_s, state_gdn_conv[:, l], consts, sgu_s, db, ts)
        hpa_s3 = hpa_s.reshape(db, ts, W_A)
        ya_s = _moba_sample(pad8(hpa_s3[:, :, :512]), pad8(hpa_s3[:, :, 512:1024]),
                            pad8(hpa_s3[:, :, 1024:1536]), ckt, cvt, pt_flat, l, db, n_pages, ts)
        ya_s = ya_s[:, :ts].reshape(ns, A_WIDTH)
        gdn_in = [pad8(a) for a in (qn_s, kn_s, v_s, bx_s, gx_s)]
        yb_s, ss = _gdn(*gdn_in, _state_to_pairs(state_gdn[:, l]), 8, 4, 1)
        yb_s = yb_s[:, :ts].reshape(ns, 512)
        x1s = _merge(ya_s, yb_s, hpa_s, yc_s.reshape(ns, 512), hpc_s, ys, bd, ng, woa, wob, woc, wmo,
                     lg[0], lb[0])
        qx_s = _matmul(x1s, wxq, 512, 512)
        ox_s = _xattn_cache(pad8(qx_s), cache_mem_k, cache_mem_v, l)
        ox_s = ox_s[:, :ts].reshape(ns, D_MODEL)
        x2 = _proj_res_ln_pair(ox, x1, ox_s, x1s, wxo, lg[1], lb[1], _tile(ns, 512))
        outs["ks"].append(hpa_s3[:, :, 512:1024].reshape(db, ts, A_HEADS, A_HEAD_DIM))
        outs["vs"].append(hpa_s3[:, :, 1024:1536].reshape(db, ts, A_HEADS, A_HEAD_DIM))
        outs["cs"].append(hpa_s3[:, ts - (CONV_W - 1):, 1536:3072])
        outs["ss"].append(_state_from_pairs(ss))
        outs["vr"].append(vr_s.reshape(db, ts, C_WIDTH))

        yp, ys = _moe(x2, np_, wr_pad, br_pad, w_gate_up, b_gate_up, w_down, b_down, l, lg[2], lb[2])

    st = lambda k: jnp.stack(outs[k], axis=1)
    return (yp.reshape(bp, seq, D_MODEL), ys.reshape(db, ts, D_MODEL), st("kp"), st("vp"), st("ks"),
            st("vs"), st("sp"), st("ss"), st("cp"), st("cs"), st("mk"), st("mv"), st("vr"))
```

```python
import functools

import jax
import jax.numpy as jnp
from jax import lax
from jax.experimental import pallas as pl
from jax.experimental.pallas import tpu as pltpu

F32 = jnp.float32
BF16 = jnp.bfloat16

D_MODEL = 1024
DEPTH = 2
A_HEADS = 8
A_HEAD_DIM = 64
A_WIDTH = 512
MOBA_BLOCK = 256
MOBA_TOPK = 3
B_HEADS = 8
B_DK = 64
B_WIDTH = 512
B_CONV_CH = 1536
CONV_W = 4
GDN_CHUNK = 64
C_GROUPS = 4
C_CHUNK = 128
C_WIDTH = 512
X_HEADS = 4
X_HEAD_DIM = 256
N_EXPERTS = 32
TOP_K = 4
D_FF = 1024
SWIGLU_LIMIT = 7.0
SWIGLU_ALPHA = 1.702
DN_ALPHA = (2 * DEPTH) ** 0.25
LN_EPS = 1e-5
RMS_EPS = 1e-6

OFF_BZ = 3072
OFF_BB = 3584
OFF_CU = 3600
P_IN = 7696
W_A = 3584
W_C = P_IN - OFF_CU
LANES = 128
HEAD_PAIRS = B_WIDTH // LANES
MOE_BLK = 256
KV_GROUP = 4
NEG = -1e30
VMEM_LIMIT = 56 * 1024 * 1024


def _cp(sem, vmem=VMEM_LIMIT):
    return pltpu.CompilerParams(dimension_semantics=sem, vmem_limit_bytes=vmem)


def _tile(m, pref):
    if m <= pref:
        return m
    t = pref - pref % 8
    while m % t:
        t -= 8
    return t


def _bf(x):
    return x.astype(BF16)


def _dot(a, b):
    return jnp.dot(_bf(a), _bf(b), preferred_element_type=F32)


_NT = (((1,), (1,)), ((), ()))


def _dot_nt(a, b):
    return lax.dot_general(_bf(a), _bf(b), _NT, preferred_element_type=F32)


def _split(a, n):
    parts, r = [], a
    for _ in range(n):
        p = r.astype(BF16)
        parts.append(p)
        r = r - p.astype(F32)
    return parts


def _dot_x(a, b_bf, n):
    acc = None
    for p in _split(a, n):
        d = jnp.dot(p, b_bf, preferred_element_type=F32)
        acc = d if acc is None else acc + d
    return acc


def _xdot(b_bf, a, n, nt=False):
    acc = None
    for p in _split(a, n):
        if nt:
            d = lax.dot_general(b_bf, p, _NT, preferred_element_type=F32)
        else:
            d = jnp.dot(b_bf, p, preferred_element_type=F32)
        acc = d if acc is None else acc + d
    return acc


def _dot_nt_hl(a, b):
    ah, al = _split(a, 2)
    bh, bl = _split(b, 2)
    f = lambda x, y: lax.dot_general(x, y, _NT, preferred_element_type=F32)
    return f(ah, bh) + f(ah, bl) + f(al, bh)


def _dot_hl(a, b):
    ah, al = _split(a, 2)
    bh, bl = _split(b, 2)
    f = lambda x, y: jnp.dot(x, y, preferred_element_type=F32)
    return f(ah, bh) + f(ah, bl) + f(al, bh)


def _iota(shape, dim):
    return lax.broadcasted_iota(jnp.int32, shape, dim)


def _ln(x, g, b):
    mu = jnp.mean(x, axis=-1, keepdims=True)
    xc = x - mu
    var = jnp.mean(xc * xc, axis=-1, keepdims=True)
    return xc * lax.rsqrt(var + LN_EPS) * g + b


def _softplus(x):
    return jnp.maximum(x, 0.0) + jnp.log(1.0 + jnp.exp(-jnp.abs(x)))


def _top_mask(g, idx, n_idx, k):
    sel = jnp.zeros(g.shape, F32)
    for _ in range(k):
        m = jnp.max(g, axis=-1, keepdims=True)
        first = jnp.min(jnp.where(g == m, idx, float(n_idx)), axis=-1, keepdims=True)
        pick = (idx == first) & (m > -jnp.inf)
        sel = jnp.where(pick, 1.0, sel)
        g = jnp.where(pick, -jnp.inf, g)
    return sel


def _mm_kernel(x_ref, w_ref, o_ref, xb_ref):
    @pl.when(pl.program_id(1) == 0)
    def _():
        xb_ref[...] = _bf(x_ref[...])
    o_ref[...] = jnp.dot(xb_ref[...], w_ref[...], preferred_element_type=F32)


def _matmul(x, w_bf, tm, tn):
    m, k = x.shape
    n = w_bf.shape[1]
    tm, tn = _tile(m, tm), min(tn, n)
    return pl.pallas_call(
        _mm_kernel, grid=(m // tm, n // tn),
        in_specs=[pl.BlockSpec((tm, k), lambda i, j: (i, 0)),
                  pl.BlockSpec((k, tn), lambda i, j: (0, j))],
        out_specs=pl.BlockSpec((tm, tn), lambda i, j: (i, j)),
        out_shape=jax.ShapeDtypeStruct((m, n), F32),
        scratch_shapes=[pltpu.VMEM((tm, k), BF16)],
        compiler_params=_cp(("parallel", "arbitrary")), name="matmul")(x, w_bf)


def _proj_ln2_kernel(a1_ref, r1_ref, a2_ref, r2_ref, w_ref, g_ref, b_ref, o_ref, *, n_first):
    def emit(a_ref, r_ref):
        y = jnp.dot(_bf(a_ref[...]), w_ref[...], preferred_element_type=F32)
        o_ref[...] = _ln(DN_ALPHA * r_ref[...] + y, g_ref[...], b_ref[...])

    @pl.when(pl.program_id(0) < n_first)
    def _():
        emit(a1_ref, r1_ref)

    @pl.when(pl.program_id(0) >= n_first)
    def _():
        emit(a2_ref, r2_ref)


def _proj_res_ln_pair(a1, res1, a2, res2, w_bf, g, b, tm):
    m1, k = a1.shape
    m2 = a2.shape[0]
    n1, n2 = m1 // tm, m2 // tm
    first = lambda i: (jnp.minimum(i, n1 - 1), 0)
    second = lambda i: (jnp.maximum(i - n1, 0), 0)
    fix = lambda i: (0, 0)
    return pl.pallas_call(
        functools.partial(_proj_ln2_kernel, n_first=n1), grid=(n1 + n2,),
        in_specs=[pl.BlockSpec((tm, k), first), pl.BlockSpec((tm, D_MODEL), first),
                  pl.BlockSpec((tm, k), second), pl.BlockSpec((tm, D_MODEL), second),
                  pl.BlockSpec((k, D_MODEL), fix), pl.BlockSpec((1, D_MODEL), fix),
                  pl.BlockSpec((1, D_MODEL), fix)],
        out_specs=pl.BlockSpec((tm, D_MODEL), lambda i: (i, 0)),
        out_shape=jax.ShapeDtypeStruct((m1 + m2, D_MODEL), F32),
        compiler_params=_cp(("arbitrary",)), name="proj_res_ln")(a1, res1, a2, res2, w_bf, g, b)


def _gdn_pre(c, hs, bd, eb, eg, alog, dtb):
    q = c[:, :512]
    k = c[:, 512:1024]
    v = c[:, 1024:]
    qn = q * lax.rsqrt(_dot_x(q * q, bd, 2) + RMS_EPS) * (B_DK ** -0.5)
    kn = k * lax.rsqrt(_dot_x(k * k, bd, 2) + RMS_EPS)
    beta = jax.nn.sigmoid(hs)
    g = -jnp.exp(alog) * _softplus(hs + dtb)
    return qn, kn, v, _dot_x(beta, eb, 3), _dot_x(g, eg, 3)


def _post_p_kernel(bqkv_ref, hs_ref, cuv_ref, cw_ref, bd_ref, eb_ref, eg_ref, alog_ref, dtb_ref,
                   sg_ref, sb_ref, ws_ref, bx_ref,
                   qn_ref, kn_ref, v_ref, beta_ref, g_ref, yc_ref, xe_ref, *, tm):
    @pl.when(pl.program_id(1) == 0)
    def _():
        xe_ref[0:8, :] = jnp.zeros((8, B_CONV_CH), F32)
    xin = bqkv_ref[...]
    xe_ref[8:8 + tm, :] = xin
    cw = cw_ref[...]
    acc = xe_ref[5:5 + tm, :] * cw[0:1]
    acc = acc + xe_ref[6:6 + tm, :] * cw[1:2]
    acc = acc + xe_ref[7:7 + tm, :] * cw[2:3]
    acc = acc + xin * cw[3:4]
    xe_ref[0:8, :] = xe_ref[tm:tm + 8, :]
    c = acc * jax.nn.sigmoid(acc)
    qn, kn, v, bx, gx = _gdn_pre(c, hs_ref[...], bd_ref[...], eb_ref[...], eg_ref[...],
                                 alog_ref[...], dtb_ref[...])
    qn_ref[...] = qn
    kn_ref[...] = kn
    v_ref[...] = v
    beta_ref[...] = bx
    g_ref[...] = gx
    u = jax.nn.gelu(cuv_ref[:, :C_WIDTH])
    vr = _bf(_ln(jax.nn.gelu(cuv_ref[:, C_WIDTH:]), sg_ref[...], sb_ref[...]))
    for ch in range(tm // C_CHUNK):
        rs = slice(ch * C_CHUNK, (ch + 1) * C_CHUNK)
        for gi in range(C_GROUPS):
            cs = slice(gi * LANES, (gi + 1) * LANES)
            mixed = jnp.dot(ws_ref[gi], vr[rs, cs], preferred_element_type=F32) + bx_ref[:, cs]
            yc_ref[rs, cs] = u[rs, cs] * mixed


def _post_prompt(hpa, hps, hpc, consts, sgu, bsz, seq, tm=256):
    n = bsz * seq
    nt = seq // tm
    row = lambda b, t: (b * nt + t, 0)
    fix2 = lambda b, t: (0, 0)
    fix3 = lambda b, t: (0, 0, 0)
    cw, bd, eb, eg, alog, dtb = consts
    sg, sb, ws, bx = sgu
    o512 = jax.ShapeDtypeStruct((n, 512), F32)
    return pl.pallas_call(
        functools.partial(_post_p_kernel, tm=tm), grid=(bsz, nt),
        in_specs=[pl.BlockSpec((tm, B_CONV_CH), lambda b, t: (b * nt + t, 1)),
                  pl.BlockSpec((tm, LANES), row),
                  pl.BlockSpec((tm, 2 * C_WIDTH), row),
                  pl.BlockSpec((CONV_W, B_CONV_CH), fix2),
                  pl.BlockSpec((512, 512), fix2), pl.BlockSpec((LANES, 512), fix2),
                  pl.BlockSpec((LANES, 512), fix2), pl.BlockSpec((1, LANES), fix2),
                  pl.BlockSpec((1, LANES), fix2),
                  pl.BlockSpec((1, C_WIDTH), fix2), pl.BlockSpec((1, C_WIDTH), fix2),
                  pl.BlockSpec((C_GROUPS, C_CHUNK, C_CHUNK), fix3),
                  pl.BlockSpec((C_CHUNK, C_WIDTH), fix2)],
        out_specs=[pl.BlockSpec((tm, 512), row)] * 6,
        out_shape=[o512] * 6,
        scratch_shapes=[pltpu.VMEM((tm + 8, B_CONV_CH), F32)],
        compiler_params=_cp(("parallel", "arbitrary")), name="post_prompt",
    )(hpa, hps, hpc, cw, bd, eb, eg, alog, dtb, sg, sb, ws, bx)


def _post_s_kernel(hpa_ref, hps_ref, hpc_ref, prev_ref, cw_ref, bd_ref, eb_ref, eg_ref, alog_ref,
                   dtb_ref, sg_ref, sb_ref, wr_ref, br_ref,
                   qn_ref, kn_ref, v_ref, beta_ref, g_ref, yc_ref, vr_ref, *, ts):
    cw = cw_ref[...]
    xe = [prev_ref[:, j * B_CONV_CH:(j + 1) * B_CONV_CH] for j in range(CONV_W - 1)]
    xe += [hpa_ref[:, p * W_A + 1536:p * W_A + 3072] for p in range(ts)]
    vrows = []
    for p in range(ts):
        acc = xe[p] * cw[0:1]
        for j in range(1, CONV_W):
            acc = acc + xe[p + j] * cw[j:j + 1]
        c = acc * jax.nn.sigmoid(acc)
        hs = hps_ref[:, p * LANES:(p + 1) * LANES]
        qn, kn, v, bx, gx = _gdn_pre(c, hs, bd_ref[...], eb_ref[...], eg_ref[...],
                                     alog_ref[...], dtb_ref[...])
        ps = slice(p * 512, (p + 1) * 512)
        qn_ref[:, ps] = qn
        kn_ref[:, ps] = kn
        v_ref[:, ps] = v
        beta_ref[:, ps] = bx
        g_ref[:, ps] = gx
        vr = _ln(jax.nn.gelu(hpc_ref[:, p * W_C + C_WIDTH:p * W_C + 2 * C_WIDTH]),
                 sg_ref[...], sb_ref[...])
        vr_ref[:, ps] = vr
        vrows.append(vr)
        mixed = br_ref[p:p + 1, :]
        for j in range(p + 1):
            mixed = mixed + wr_ref[p * ts + j:p * ts + j + 1, :] * vrows[j]
        yc_ref[:, ps] = jax.nn.gelu(hpc_ref[:, p * W_C:p * W_C + C_WIDTH]) * mixed


def _post_sample(hpa, hps, hpc, prev, consts, sgu, db, ts):
    cw, bd, eb, eg, alog, dtb = consts
    sg, sb, wr, br = sgu
    args = (hpa.reshape(db, ts * W_A), hps.reshape(db, ts * LANES), hpc.reshape(db, ts * W_C),
            prev.reshape(db, (CONV_W - 1) * B_CONV_CH), cw, bd, eb, eg, alog, dtb, sg, sb, wr, br)
    o = jax.ShapeDtypeStruct((db, ts * 512), F32)
    return pl.pallas_call(
        functools.partial(_post_s_kernel, ts=ts),
        out_shape=[o] * 7, compiler_params=_cp(None), name="post_sample")(*args)


def _gdn_kernel(q_ref, k_ref, v_ref, bx_ref, gx_ref, s0_ref, o_ref, sf_ref, s_scr,
                *, c_len, bb, npc, n_dbl):
    ci = pl.program_id(1)

    @pl.when(ci == 0)
    def _():
        s_scr[...] = s0_ref[...]

    c2 = 2 * c_len
    ri = _iota((c2, c2), 0)
    cj = _iota((c2, c2), 1)
    same = (ri >= c_len) == (cj >= c_len)
    low = same & (ri >= cj)
    strict = same & (ri > cj)
    eye2 = jnp.where(ri == cj, 1.0, 0.0)
    tril_c = jnp.where(_iota((c_len, c_len), 0) >= _iota((c_len, c_len), 1), 1.0, 0.0).astype(BF16)
    eye_l = jnp.where(_iota((LANES, LANES), 0) == _iota((LANES, LANES), 1), 1.0, 0.0).astype(BF16)
    hsel = jnp.where(_iota((B_HEADS, B_WIDTH), 1) == _iota((B_HEADS, B_WIDTH), 0) * B_DK,
                     1.0, 0.0).astype(BF16)
    place = (_iota((c2, LANES), 1) < B_DK) == (_iota((c2, LANES), 0) < c_len)
    top_rows = _iota((c2, 1), 0) < c_len
    left_cols = _iota((1, c2), 1) < c_len

    chains = [(i, n, p) for i in range(bb) for n in range(npc) for p in range(HEAD_PAIRS)]
    op, decay, glast = {}, {}, {}
    for i in range(bb):
        for n in range(npc):
            rs = slice(n * c_len, (n + 1) * c_len)
            q, k, v = q_ref[i, rs, :], k_ref[i, rs, :], v_ref[i, rs, :]
            bx, gx = bx_ref[i, rs, :], gx_ref[i, rs, :]
            gcb = _xdot(tril_c, gx, 3)
            g2 = jnp.concatenate([gcb, gcb], axis=0)
            gct = _xdot(hsel, g2, 3, nt=True)
            eg = jnp.exp(gcb)
            g_last = gcb[c_len - 1:c_len, :]
            gl = jnp.exp(g_last)
            kb = k * bx
            wide = (k, kb, kb * eg, v * bx, q, q * eg, k * jnp.exp(g_last - gcb))
            for p in range(HEAD_PAIRS):
                ls = slice(p * LANES, (p + 1) * LANES)
                op[(i, n, p)] = [
                    _bf(jnp.where(place, jnp.concatenate([x[:, ls], x[:, ls]], axis=0), 0.0))
                    for x in wide]
                gcol = jnp.where(top_rows, g2[:, p * LANES:p * LANES + 1],
                                 g2[:, p * LANES + B_DK:p * LANES + B_DK + 1])
                grow = jnp.where(left_cols, gct[2 * p:2 * p + 1, :], gct[2 * p + 1:2 * p + 2, :])
                decay[(i, n, p)] = jnp.where(low, jnp.exp(jnp.where(low, gcol - grow, 0.0)), 0.0)
                glast[(i, n, p)] = gl[:, ls]
    mm = lambda x, y: jnp.dot(x, y, preferred_element_type=F32)
    mm_nt = lambda x, y: lax.dot_general(x, y, _NT, preferred_element_type=F32)
    kk = {c: mm_nt(op[c][1], op[c][0]) for c in chains}
    qk = {c: mm_nt(op[c][4], op[c][0]) for c in chains}
    kd_t = {c: _bf(mm_nt(eye_l, op[c][6])) for c in chains}
    a_mat = {c: jnp.where(strict, kk[c] * decay[c], 0.0) for c in chains}
    qk = {c: _bf(qk[c] * decay[c]) for c in chains}
    t_mat = {c: eye2 - a_mat[c] for c in chains}
    pw = a_mat
    for _ in range(n_dbl):
        pwb = {c: _bf(pw[c]) for c in chains}
        pw = {c: mm(pwb[c], pwb[c]) for c in chains}
        tp = {c: mm(_bf(t_mat[c]), _bf(pw[c])) for c in chains}
        t_mat = {c: t_mat[c] + tp[c] for c in chains}
    tb = {c: _bf(t_mat[c]) for c in chains}
    u = {c: mm(tb[c], op[c][3]) for c in chains}
    w = {c: _bf(mm(tb[c], op[c][2])) for c in chains}

    pairs = [(i, p) for i in range(bb) for p in range(HEAD_PAIRS)]
    state = {ip: s_scr[ip[0], ip[1]] for ip in pairs}
    for n in range(npc):
        cn = {ip: (ip[0], n, ip[1]) for ip in pairs}
        sb = {ip: _bf(state[ip]) for ip in pairs}
        ws = {ip: mm(w[cn[ip]], sb[ip]) for ip in pairs}
        qs_ = {ip: mm(op[cn[ip]][5], sb[ip]) for ip in pairs}
        vb = {ip: _bf(u[cn[ip]] - ws[ip]) for ip in pairs}
        ov = {ip: mm(qk[cn[ip]], vb[ip]) for ip in pairs}
        sv = {ip: mm(kd_t[cn[ip]], vb[ip]) for ip in pairs}
        for ip in pairs:
            o = qs_[ip] + ov[ip]
            o_ref[ip[0], n * c_len:(n + 1) * c_len, ip[1] * LANES:(ip[1] + 1) * LANES] = (
                o[:c_len] + o[c_len:])
            state[ip] = state[ip] * glast[cn[ip]] + sv[ip]
    for ip in pairs:
        s_scr[ip[0], ip[1]] = state[ip]

    @pl.when(ci == pl.num_programs(1) - 1)
    def _():
        sf_ref[...] = s_scr[...]


def _gdn(qn, kn, v, bx, gx, s0_bd, c_len, bb, npc):
    bsz, t, _ = qn.shape
    n_dbl = max(c_len.bit_length() - 2, 0)
    tile = npc * c_len
    row = pl.BlockSpec((bb, tile, 512), lambda b, c: (b, c, 0))
    st = pl.BlockSpec((bb, HEAD_PAIRS, LANES, LANES), lambda b, c: (b, 0, 0, 0))
    return pl.pallas_call(
        functools.partial(_gdn_kernel, c_len=c_len, bb=bb, npc=npc, n_dbl=n_dbl),
        grid=(bsz // bb, t // tile),
        in_specs=[row] * 5 + [st], out_specs=[row, st],
        out_shape=[jax.ShapeDtypeStruct((bsz, t, 512), F32),
                   jax.ShapeDtypeStruct((bsz, HEAD_PAIRS, LANES, LANES), F32)],
        scratch_shapes=[pltpu.VMEM((bb, HEAD_PAIRS, LANES, LANES), F32)],
        compiler_params=_cp(("parallel", "arbitrary")), name="gdn",
    )(qn, kn, v, bx, gx, s0_bd)


def _state_to_pairs(s):
    b = s.shape[0]
    s5 = s.reshape(b, HEAD_PAIRS, 2, B_DK, B_DK)
    z = jnp.zeros_like(s5[:, :, 0])
    top = jnp.concatenate([s5[:, :, 0], z], axis=-1)
    bot = jnp.concatenate([z, s5[:, :, 1]], axis=-1)
    return jnp.concatenate([top, bot], axis=-2)


def _state_from_pairs(sp):
    b = sp.shape[0]
    return jnp.stack([sp[:, :, :B_DK, :B_DK], sp[:, :, B_DK:, B_DK:]], axis=2).reshape(
        b, B_HEADS, B_DK, B_DK)


def _moba_p_kernel(q_ref, k_ref, v_ref, o_ref, km_ref, ka_ref, vb_ref, *, nb, n_sel, grp):
    own = pl.program_id(2)
    tq = MOBA_BLOCK
    heads = range(2)

    @pl.when(own == 0)
    def _():
        km_ref[...] = jnp.zeros(km_ref.shape, F32)
        lane = _iota((MOBA_BLOCK, LANES), 1)
        for n in range(nb):
            rows = slice(n * MOBA_BLOCK, (n + 1) * MOBA_BLOCK)
            kf = k_ref[0, rows, :]
            km_ref[n:n + 1, :] = jnp.mean(kf, axis=0, keepdims=True)
            vb_ref[rows, :] = _bf(v_ref[0, rows, :])
            ka_ref[rows, :LANES] = _bf(kf)
            ka_ref[rows, LANES:] = jnp.where(lane == n, 1.0, 0.0).astype(BF16)

    q2 = q_ref[0]
    lane = _iota((tq, LANES), 1)
    lane_f = lane.astype(F32)
    first_head = lane < A_HEAD_DIM
    scale = A_HEAD_DIM ** -0.5
    km = km_ref[...]
    qhb, qaug = [], []
    for h in heads:
        qh = jnp.where(first_head == (h == 0), q2, 0.0)
        g = _dot_nt_hl(qh, km)
        g = jnp.where(lane < own, g, -jnp.inf)
        sel = _top_mask(g, lane_f, LANES, n_sel)
        bias = jnp.where((lane < nb) & (sel == 0.0), NEG, 0.0)
        qhb.append(_bf(qh * scale))
        qaug.append(jnp.concatenate([qhb[h], _bf(bias)], axis=1))

    start = pl.multiple_of(own * MOBA_BLOCK, MOBA_BLOCK)
    k_own = _bf(k_ref[0, pl.ds(start, MOBA_BLOCK), :])
    v_own = vb_ref[pl.ds(start, MOBA_BLOCK), :]
    causal = _iota((tq, MOBA_BLOCK), 1) <= _iota((tq, MOBA_BLOCK), 0)
    s_own = [jnp.where(causal, lax.dot_general(qhb[h], k_own, _NT, preferred_element_type=F32), NEG)
             for h in heads]
    span = grp * MOBA_BLOCK
    n_grp = (own + grp - 1) // grp

    def group_scores(gi):
        st = pl.multiple_of(gi * span, span)
        kg = ka_ref[pl.ds(st, span), :]
        return st, [lax.dot_general(qaug[h], kg, _NT, preferred_element_type=F32) for h in heads]

    def body(gi, carry):
        st, ss = group_scores(gi)
        vg = vb_ref[pl.ds(st, span), :]
        m_new = [jnp.maximum(carry[3 * h], jnp.max(ss[h], axis=-1, keepdims=True)) for h in heads]
        ps = [jnp.exp(ss[h] - m_new[h]) for h in heads]
        pv = [jnp.dot(_bf(ps[h]), vg, preferred_element_type=F32) for h in heads]
        out = []
        for h in heads:
            alpha = jnp.exp(carry[3 * h] - m_new[h])
            l = carry[3 * h + 1] * alpha + jnp.sum(ps[h], axis=-1, keepdims=True)
            out += [m_new[h], l, carry[3 * h + 2] * alpha + pv[h]]
        return tuple(out)

    init = []
    for h in heads:
        m = jnp.max(s_own[h], axis=-1, keepdims=True)
        p = jnp.exp(s_own[h] - m)
        init += [m, jnp.sum(p, axis=-1, keepdims=True),
                 jnp.dot(_bf(p), v_own, preferred_element_type=F32)]
    _, l0, a0, _, l1, a1 = lax.fori_loop(0, n_grp, body, tuple(init))
    o_ref[0] = jnp.where(first_head, a0 / l0, a1 / l1)


def _moba_prompt(hpa3, bsz, seq):
    nb = seq // MOBA_BLOCK
    assert nb <= LANES, "one block-indicator lane per key block"
    grp = KV_GROUP if nb % KV_GROUP == 0 else 1
    npair = A_WIDTH // LANES
    return pl.pallas_call(
        functools.partial(_moba_p_kernel, nb=nb, n_sel=min(MOBA_TOPK, nb - 1), grp=grp),
        grid=(bsz, npair, nb),
        in_specs=[pl.BlockSpec((1, MOBA_BLOCK, LANES), lambda b, p, i: (b, i, p)),
                  pl.BlockSpec((1, seq, LANES), lambda b, p, i: (b, 0, npair + p)),
                  pl.BlockSpec((1, seq, LANES), lambda b, p, i: (b, 0, 2 * npair + p))],
        out_specs=pl.BlockSpec((1, MOBA_BLOCK, LANES), lambda b, p, i: (b, i, p)),
        out_shape=jax.ShapeDtypeStruct((bsz, seq, A_WIDTH), F32),
        scratch_shapes=[pltpu.VMEM((LANES, LANES), F32), pltpu.VMEM((seq, 2 * LANES), BF16),
                        pltpu.VMEM((seq, LANES), BF16)],
        compiler_params=_cp(("parallel", "parallel", "arbitrary")), name="moba_prompt",
    )(hpa3, hpa3, hpa3)


def _moba_s_kernel(pt_ref, q_ref, kn_ref, vn_ref, *refs, n_pages, ts, page):
    kt_pages = refs[:n_pages]
    vt_pages = refs[n_pages:2 * n_pages]
    o_ref, s_ref = refs[2 * n_pages:]
    ppb = MOBA_BLOCK // page
    nb = n_pages // ppb
    rows = A_HEADS * 8
    q8 = q_ref[0]
    lane_head = _iota((8, A_WIDTH), 1) // A_HEAD_DIM
    qbd = jnp.concatenate([jnp.where(lane_head == h, q8, 0.0) for h in range(A_HEADS)], axis=0)
    qbd_b = _bf(qbd)
    scale = A_HEAD_DIM ** -0.5
    col = _iota((A_WIDTH, LANES), 1)
    km = jnp.zeros((A_WIDTH, LANES), F32)
    for n in range(nb):
        tot = None
        for pp in range(ppb):
            sm = jnp.sum(kt_pages[n * ppb + pp][0, 0], axis=1, keepdims=True)
            tot = sm if tot is None else tot + sm
        km = jnp.where(col == n, tot * (1.0 / MOBA_BLOCK), km)
    blk = _iota((rows, LANES), 1)
    g = _dot_hl(qbd, km)
    g = jnp.where(blk < nb, g, -jnp.inf)
    sel = _top_mask(g, blk.astype(F32), LANES, min(MOBA_TOPK, nb))
    s_own = lax.dot_general(qbd_b, _bf(kn_ref[0]), _NT, preferred_element_type=F32) * scale
    kr = _iota((rows, 8), 1)
    qr = _iota((rows, 8), 0) % 8
    s_own = jnp.where((kr <= qr) & (kr < ts), s_own, NEG)
    m_tile = jnp.full((rows, page), NEG, F32)
    for pg in range(n_pages):
        s = jnp.dot(qbd_b, _bf(kt_pages[pg][0, 0]), preferred_element_type=F32) * scale
        s = jnp.where(sel[:, pg // ppb:pg // ppb + 1] > 0.0, s, NEG)
        s_ref[:, pg * page:(pg + 1) * page] = s
        m_tile = jnp.maximum(m_tile, s)
    m = jnp.maximum(jnp.max(s_own, axis=-1, keepdims=True), jnp.max(m_tile, axis=-1, keepdims=True))
    p_own = jnp.exp(s_own - m)
    l_tile = jnp.zeros((rows, page), F32)
    for pg in range(n_pages):
        p = jnp.exp(s_ref[:, pg * page:(pg + 1) * page] - m)
        s_ref[:, pg * page:(pg + 1) * page] = p
        l_tile = l_tile + p
    l = jnp.sum(p_own, axis=-1, keepdims=True) + jnp.sum(l_tile, axis=-1, keepdims=True)
    inv = 1.0 / l
    o = jnp.dot(_bf(p_own * inv), _bf(vn_ref[0]), preferred_element_type=F32)
    for pg in range(n_pages):
        p = s_ref[:, pg * page:(pg + 1) * page] * inv
        o = o + lax.dot_general(_bf(p), _bf(vt_pages[pg][0, 0]), _NT, preferred_element_type=F32)
    out = jnp.zeros((8, A_WIDTH), F32)
    for h in range(A_HEADS):
        out = out + jnp.where(lane_head == h, o[h * 8:(h + 1) * 8, :], 0.0)
    o_ref[0] = out


def _moba_sample(q8, kn8, vn8, cache_kt, cache_vt, pt_flat, layer, db, n_pages, ts):
    page = cache_kt.shape[3]
    tok = pl.BlockSpec((1, 8, A_WIDTH), lambda b, pt: (b, 0, 0))

    def page_spec(p):
        return pl.BlockSpec((1, 1, A_WIDTH, page), lambda b, pt: (pt[b * n_pages + p], layer, 0, 0))

    specs = [page_spec(p) for p in range(n_pages)]
    grid_spec = pltpu.PrefetchScalarGridSpec(
        num_scalar_prefetch=1, grid=(db,),
        in_specs=[tok, tok, tok] + specs + specs,
        out_specs=tok,
        scratch_shapes=[pltpu.VMEM((A_HEADS * 8, n_pages * page), F32)])
    return pl.pallas_call(
        functools.partial(_moba_s_kernel, n_pages=n_pages, ts=ts, page=page),
        grid_spec=grid_spec, out_shape=jax.ShapeDtypeStruct((db, 8, A_WIDTH), F32),
        compiler_params=_cp(("parallel",)), name="moba_sample",
    )(pt_flat, q8, kn8, vn8, *([cache_kt] * n_pages), *([cache_vt] * n_pages))


def _merge_kernel(ya_ref, yb_ref, z_ref, yc_ref, g0_ref, g1_ref, g2_ref, x_ref, bd_ref, ng_ref,
                  woa_ref, wob_ref, woc_ref, wmo_ref, lg_ref, lb_ref, o_ref):
    f = lambda y, w: jnp.dot(_bf(y), w[...], preferred_element_type=F32)
    yb = yb_ref[...]
    z = z_ref[...]
    ms = _dot_x(yb * yb, bd_ref[...], 2) * (1.0 / B_DK)
    yb = yb * lax.rsqrt(ms + RMS_EPS) * ng_ref[...] * (z * jax.nn.sigmoid(z))
    merged = jax.nn.sigmoid(g0_ref[...]) * f(ya_ref[...], woa_ref)
    merged = merged + jax.nn.sigmoid(g1_ref[...]) * f(yb, wob_ref)
    merged = merged + jax.nn.sigmoid(g2_ref[...]) * f(yc_ref[...], woc_ref)
    y = jnp.dot(_bf(merged), wmo_ref[...], preferred_element_type=F32)
    o_ref[...] = _ln(DN_ALPHA * x_ref[...] + y, lg_ref[...], lb_ref[...])


def _merge(ya, yb, hpa, yc, hpc, x, bd, ng, woa, wob, woc, wmo, lg, lb, tm=256):
    m = x.shape[0]
    tm = _tile(m, tm)
    row = lambda i: (i, 0)
    fix = lambda i: (0, 0)
    y512 = pl.BlockSpec((tm, 512), row)
    gate = lambda j: pl.BlockSpec((tm, D_MODEL), lambda i: (i, j))
    wsm = pl.BlockSpec((512, D_MODEL), fix)
    vec = pl.BlockSpec((1, D_MODEL), fix)
    return pl.pallas_call(
        _merge_kernel, grid=(m // tm,),
        in_specs=[y512, y512, pl.BlockSpec((tm, 512), lambda i: (i, OFF_BZ // 512)), y512,
                  gate(1), gate(2), gate(3), pl.BlockSpec((tm, D_MODEL), row),
                  pl.BlockSpec((512, 512), fix), pl.BlockSpec((1, 512), fix),
                  wsm, wsm, wsm, pl.BlockSpec((D_MODEL, D_MODEL), fix), vec, vec],
        out_specs=pl.BlockSpec((tm, D_MODEL), row),
        out_shape=jax.ShapeDtypeStruct((m, D_MODEL), F32),
        compiler_params=_cp(("parallel",)), name="merge",
    )(ya, yb, hpa, yc, hpc, hpc, hpc, x, bd, ng, woa, wob, woc, wmo, lg, lb)


def _attend(q, k, v):
    s = _dot_nt(q, k) * (X_HEAD_DIM ** -0.5)
    p = jnp.exp(s - jnp.max(s, axis=-1, keepdims=True))
    p = p / jnp.sum(p, axis=-1, keepdims=True)
    return _dot(p, v)


def _xattn_kernel(q_ref, k_ref, v_ref, o_ref):
    for h in range(X_HEADS):
        cs = slice(h * X_HEAD_DIM, (h + 1) * X_HEAD_DIM)
        o_ref[0, :, cs] = _attend(q_ref[0, :, cs], k_ref[0, :, cs], v_ref[0, :, cs])


def _xattn(q3, k3, v3, tm):
    g, r, _ = q3.shape
    tm = _tile(r, tm)
    mem = k3.shape[1]
    qs = pl.BlockSpec((1, tm, D_MODEL), lambda i, t: (i, t, 0))
    ms = pl.BlockSpec((1, mem, D_MODEL), lambda i, t: (i, 0, 0))
    return pl.pallas_call(
        _xattn_kernel, grid=(g, r // tm), in_specs=[qs, ms, ms], out_specs=qs,
        out_shape=jax.ShapeDtypeStruct(q3.shape, F32),
        compiler_params=_cp(("parallel", "parallel")), name="xattn")(q3, k3, v3)


def _xattn_cache_kernel(q_ref, k_hbm, v_hbm, o_ref, kbuf, vbuf, sem, *, nseq, layer):
    units = [(g, h) for g in range(nseq) for h in range(X_HEADS)]
    base = pl.program_id(0) * nseq

    def head_copies(g, h):
        return (pltpu.make_async_copy(k_hbm.at[base + g, layer, :, h, :], kbuf.at[g, h], sem.at[0]),
                pltpu.make_async_copy(v_hbm.at[base + g, layer, :, h, :], vbuf.at[g, h], sem.at[1]))

    for u in units:
        for c in head_copies(*u):
            c.start()
    for u in units:
        for c in head_copies(*u):
            c.wait()
    cs = lambda h: slice(h * X_HEAD_DIM, (h + 1) * X_HEAD_DIM)
    scale = X_HEAD_DIM ** -0.5
    s = {u: _dot_nt(q_ref[u[0], :, cs(u[1])] * scale, kbuf[u[0], u[1]]) for u in units}
    p = {u: jnp.exp(s[u] - jnp.max(s[u], axis=-1, keepdims=True)) for u in units}
    p = {u: p[u] / jnp.sum(p[u], axis=-1, keepdims=True) for u in units}
    o = {u: _dot(p[u], vbuf[u[0], u[1]]) for u in units}
    for g, h in units:
        o_ref[g, :, cs(h)] = o[(g, h)]


def _xattn_cache(q3, cache_k, cache_v, layer, nseq=4):
    g = q3.shape[0]
    mem = cache_k.shape[2]
    qs = pl.BlockSpec((nseq, 8, D_MODEL), lambda i: (i, 0, 0))
    anyspec = pl.BlockSpec(memory_space=pl.ANY)
    buf = pltpu.VMEM((nseq, X_HEADS, mem, X_HEAD_DIM), F32)
    return pl.pallas_call(
        functools.partial(_xattn_cache_kernel, nseq=nseq, layer=layer), grid=(g // nseq,),
        in_specs=[qs, anyspec, anyspec], out_specs=qs,
        out_shape=jax.ShapeDtypeStruct(q3.shape, F32),
        scratch_shapes=[buf, buf, pltpu.SemaphoreType.DMA((2,))],
        compiler_params=_cp(("arbitrary",)), name="xattn_cache")(q3, cache_k, cache_v)


def _router_kernel(x_ref, wr_ref, br_ref, tri_ref, idx_ref, gate_ref, rank_ref, cnt_ref, carry_ref):
    @pl.when(pl.program_id(0) == 0)
    def _():
        carry_ref[...] = jnp.zeros(carry_ref.shape, F32)

    logits = _dot_hl(x_ref[...], wr_ref[...]) + br_ref[...]
    tm = logits.shape[0]
    lane = _iota((tm, LANES), 1)
    lane_f = lane.astype(F32)
    lg = logits
    onehots, tops, firsts = [], [], []
    for _ in range(TOP_K):
        m = jnp.max(lg, axis=-1, keepdims=True)
        first = jnp.min(jnp.where(lg == m, lane_f, float(LANES)), axis=-1, keepdims=True)
        pick = lane_f == first
        lg = jnp.where(pick, -jnp.inf, lg)
        onehots.append(jnp.where(pick, 1.0, 0.0))
        tops.append(m)
        firsts.append(first)
    es = [jnp.exp(t - tops[0]) for t in tops]
    den = es[0] + es[1] + es[2] + es[3]
    all_hot = onehots[0] + onehots[1] + onehots[2] + onehots[3]
    before = jnp.dot(tri_ref[...], _bf(all_hot), preferred_element_type=F32) + carry_ref[...]
    idx_o = jnp.zeros((tm, LANES), F32)
    gate_o = jnp.zeros((tm, LANES), F32)
    rank_o = jnp.zeros((tm, LANES), F32)
    for k in range(TOP_K):
        rk = jnp.sum(onehots[k] * before, axis=-1, keepdims=True)
        idx_o = jnp.where(lane == k, firsts[k], idx_o)
        gate_o = jnp.where(lane == k, es[k] / den, gate_o)
        rank_o = jnp.where(lane == k, rk, rank_o)
    idx_ref[...] = idx_o.astype(jnp.int32)
    gate_ref[...] = gate_o
    rank_ref[...] = rank_o.astype(jnp.int32)
    carry_ref[...] = carry_ref[...] + jnp.sum(all_hot, axis=0, keepdims=True)
    cnt_ref[...] = carry_ref[...].astype(jnp.int32)


def _router(x, wr_pad, br_pad, tm=512):
    n = x.shape[0]
    tm = _tile(n, tm)
    tri = jnp.tril(jnp.ones((tm, tm), F32), -1).astype(BF16)
    row = lambda i: (i, 0)
    fix = lambda i: (0, 0)
    o = pl.BlockSpec((tm, LANES), row)
    return pl.pallas_call(
        _router_kernel, grid=(n // tm,),
        in_specs=[pl.BlockSpec((tm, D_MODEL), row), pl.BlockSpec((D_MODEL, LANES), fix),
                  pl.BlockSpec((1, LANES), fix), pl.BlockSpec((tm, tm), fix)],
        out_specs=[o, o, o, pl.BlockSpec((1, LANES), fix)],
        out_shape=[jax.ShapeDtypeStruct((n, LANES), jnp.int32), jax.ShapeDtypeStruct((n, LANES), F32),
                   jax.ShapeDtypeStruct((n, LANES), jnp.int32),
                   jax.ShapeDtypeStruct((1, LANES), jnp.int32)],
        scratch_shapes=[pltpu.VMEM((1, LANES), F32)],
        compiler_params=_cp(("arbitrary",)), name="router")(x, wr_pad, br_pad, tri)


def _dispatch_kernel(dest_ref, x_ref, xs_in_ref, xs_ref, sem, *, tm):
    del xs_in_ref

    def row_copy(t, k):
        d = dest_ref[t * TOP_K + k]
        return pltpu.make_async_copy(x_ref.at[pl.ds(t, 1), :], xs_ref.at[pl.ds(d, 1), :], sem)

    def issue(t, c):
        for k in range(TOP_K):
            row_copy(t, k).start(priority=k % 2)
        return c

    def drain(t, c):
        for k in range(TOP_K):
            row_copy(t, k).wait()
        return c

    lax.fori_loop(0, tm, issue, 0)
    lax.fori_loop(0, tm, drain, 0)


def _dispatch(x, dest_flat, xs_zero, tm=256):
    n = x.shape[0]
    return pl.pallas_call(
        functools.partial(_dispatch_kernel, tm=tm), grid=(n // tm,),
        in_specs=[pl.BlockSpec((tm * TOP_K,), lambda i: (i,), memory_space=pltpu.SMEM),
                  pl.BlockSpec((tm, D_MODEL), lambda i: (i, 0)),
                  pl.BlockSpec(memory_space=pl.ANY)],
        out_specs=pl.BlockSpec(memory_space=pl.ANY),
        out_shape=jax.ShapeDtypeStruct(xs_zero.shape, F32),
        scratch_shapes=[pltpu.SemaphoreType.DMA(())],
        input_output_aliases={2: 0},
        compiler_params=_cp(("arbitrary",)), name="moe_dispatch")(dest_flat, x, xs_zero)


def _ffn_kernel(be_ref, nu_ref, x_ref, wgu_ref, bgu_ref, wd_ref, bd_ref, o_ref, wgu_b, wd_b):
    i = pl.program_id(0)
    prev = be_ref[jnp.maximum(i - 1, 0)]
    used = i < nu_ref[0]

    @pl.when(used & ((i == 0) | (be_ref[i] != prev)))
    def _():
        wgu_b[...] = _bf(wgu_ref[0, 0])
        wd_b[...] = _bf(wd_ref[0, 0])

    @pl.when(used)
    def _():
        hgu = jnp.dot(_bf(x_ref[...]), wgu_b[...], preferred_element_type=F32) + bgu_ref[0, 0]
        gate = jnp.minimum(hgu[:, :D_FF], SWIGLU_LIMIT)
        up = jnp.clip(hgu[:, D_FF:], -SWIGLU_LIMIT, SWIGLU_LIMIT)
        act = (up + 1.0) * gate * jax.nn.sigmoid(gate * SWIGLU_ALPHA)
        o_ref[...] = jnp.dot(_bf(act), wd_b[...], preferred_element_type=F32) + bd_ref[0, 0]

    @pl.when(jnp.logical_not(used))
    def _():
        o_ref[...] = jnp.zeros(o_ref.shape, F32)


def _expert_ffn(xs, blk_e, n_used, w_gu, b_gu, w_d, b_d, layer):
    n_blocks = xs.shape[0] // MOE_BLK
    rowi = lambda i, be, nu: (jnp.minimum(i, nu[0] - 1), 0)
    rowo = lambda i, be, nu: (i, 0)
    exp4 = lambda i, be, nu: (layer, be[i], 0, 0)
    grid_spec = pltpu.PrefetchScalarGridSpec(
        num_scalar_prefetch=2, grid=(n_blocks,),
        in_specs=[pl.BlockSpec((MOE_BLK, D_MODEL), rowi),
                  pl.BlockSpec((1, 1, D_MODEL, 2 * D_FF), exp4), pl.BlockSpec((1, 1, 1, 2 * D_FF), exp4),
                  pl.BlockSpec((1, 1, D_FF, D_MODEL), exp4), pl.BlockSpec((1, 1, 1, D_MODEL), exp4)],
        out_specs=pl.BlockSpec((MOE_BLK, D_MODEL), rowo),
        scratch_shapes=[pltpu.VMEM((D_MODEL, 2 * D_FF), BF16), pltpu.VMEM((D_FF, D_MODEL), BF16)])
    return pl.pallas_call(
        _ffn_kernel, grid_spec=grid_spec, out_shape=jax.ShapeDtypeStruct(xs.shape, F32),
        compiler_params=_cp(("arbitrary",)), name="expert_ffn",
    )(blk_e, n_used, xs, w_gu, b_gu.reshape(DEPTH, N_EXPERTS, 1, -1), w_d,
      b_d.reshape(DEPTH, N_EXPERTS, 1, -1))


def _combine_kernel(dest_ref, gate_ref, x_ref, lg_ref, lb_ref, yb_ref, op_ref, os_ref,
                    buf, sem, *, tm, n_first):
    i = pl.program_id(0)

    def row_copy(t, k):
        d = dest_ref[t * TOP_K + k]
        return pltpu.make_async_copy(yb_ref.at[pl.ds(d, 1), :], buf.at[k, pl.ds(t, 1), :], sem)

    def issue(t, c):
        for k in range(TOP_K):
            row_copy(t, k).start(priority=k % 2)
        return c

    def drain(t, c):
        for k in range(TOP_K):
            row_copy(t, k).wait()
        return c

    lax.fori_loop(0, tm, issue, 0)
    lax.fori_loop(0, tm, drain, 0)
    gates = gate_ref[...]
    y = buf[0] * gates[:, 0:1]
    for k in range(1, TOP_K):
        y = y + buf[k] * gates[:, k:k + 1]
    out = _ln(DN_ALPHA * x_ref[...] + y, lg_ref[...], lb_ref[...])

    @pl.when(i < n_first)
    def _():
        op_ref[...] = out

    @pl.when(i >= n_first)
    def _():
        os_ref[...] = out


def _combine(yb, dest_flat, gates, x, lg, lb, n_first_rows, tm=256):
    n = x.shape[0]
    nt = n // tm
    n_first = n_first_rows // tm
    row = lambda i: (i, 0)
    fix = lambda i: (0, 0)
    return pl.pallas_call(
        functools.partial(_combine_kernel, tm=tm, n_first=n_first), grid=(nt,),
        in_specs=[pl.BlockSpec((tm * TOP_K,), lambda i: (i,), memory_space=pltpu.SMEM),
                  pl.BlockSpec((tm, LANES), row), pl.BlockSpec((tm, D_MODEL), row),
                  pl.BlockSpec((1, D_MODEL), fix), pl.BlockSpec((1, D_MODEL), fix),
                  pl.BlockSpec(memory_space=pl.ANY)],
        out_specs=[pl.BlockSpec((tm, D_MODEL), lambda i: (jnp.minimum(i, n_first - 1), 0)),
                   pl.BlockSpec((tm, D_MODEL), lambda i: (jnp.maximum(i - n_first, 0), 0))],
        out_shape=[jax.ShapeDtypeStruct((n_first_rows, D_MODEL), F32),
                   jax.ShapeDtypeStruct((n - n_first_rows, D_MODEL), F32)],
        scratch_shapes=[pltpu.VMEM((TOP_K, tm, D_MODEL), F32), pltpu.SemaphoreType.DMA(())],
        compiler_params=_cp(("arbitrary",)), name="moe_combine",
    )(dest_flat, gates, x, lg, lb, yb)


def _moe(x, n_first_rows, wr_pad, br_pad, w_gu, b_gu, w_d, b_d, layer, lg, lb):
    n = x.shape[0]
    idx, gates, rank, cnt = _router(x, wr_pad, br_pad)
    counts = cnt[0, :N_EXPERTS]
    padded = (counts + MOE_BLK - 1) // MOE_BLK * MOE_BLK
    pad_end = jnp.cumsum(padded)
    pad_start = pad_end - padded
    e_sel = idx[:, :TOP_K]
    dest = (pad_start[e_sel] + rank[:, :TOP_K]).reshape(-1).astype(jnp.int32)
    n_blocks = -(-(n * TOP_K) // MOE_BLK) + N_EXPERTS
    blk_e = jnp.sum(pad_end[None, :] <= (jnp.arange(n_blocks) * MOE_BLK)[:, None], axis=1)
    blk_e = jnp.minimum(blk_e, N_EXPERTS - 1).astype(jnp.int32)
    n_used = (pad_end[-1:] // MOE_BLK).astype(jnp.int32)
    xs = _dispatch(x, dest, jnp.zeros((n_blocks * MOE_BLK, D_MODEL), F32))
    yb = _expert_ffn(xs, blk_e, n_used, w_gu, b_gu, w_d, b_d, layer)
    return _combine(yb, dest, gates, x, lg, lb, n_first_rows)


def _layer_consts(l, w_in, gdn_conv_w, gdn_a_log, gdn_dt_bias, sgu_norm_g, sgu_norm_b, sgu_w, sgu_b, ts):
    wi = w_in[l]
    w_a = _bf(wi[:, :W_A])
    w_s = _bf(jnp.pad(wi[:, OFF_BB:OFF_CU], ((0, 0), (0, LANES - 2 * B_HEADS))))
    w_c = _bf(wi[:, OFF_CU:])
    lane = jnp.arange(512) // B_DK
    head = jnp.arange(LANES)
    bd = (lane[:, None] == lane[None, :]).astype(BF16)
    eb = (head[:, None] == lane[None, :]).astype(BF16)
    eg = (head[:, None] == lane[None, :] + B_HEADS).astype(BF16)
    alog = jnp.pad(gdn_a_log[l], (B_HEADS, LANES - 2 * B_HEADS)).reshape(1, LANES)
    dtb = jnp.pad(gdn_dt_bias[l], (B_HEADS, LANES - 2 * B_HEADS)).reshape(1, LANES)
    consts = (gdn_conv_w[l], bd, eb, eg, alog, dtb)
    sg, sb = sgu_norm_g[l].reshape(1, -1), sgu_norm_b[l].reshape(1, -1)
    tri = jnp.tril(jnp.ones((C_CHUNK, C_CHUNK), F32))
    ws_p = _bf(sgu_w[l] * tri)
    bx_p = jnp.repeat(sgu_b[l].T, C_CHUNK, axis=1)
    wr_s = jnp.repeat(sgu_w[l][:, :ts, :ts].transpose(1, 2, 0).reshape(ts * ts, C_GROUPS), C_CHUNK, axis=1)
    br_s = jnp.repeat(sgu_b[l][:, :ts].T, C_CHUNK, axis=1)
    return w_a, w_s, w_c, consts, (sg, sb, ws_p, bx_p), (sg, sb, wr_s, br_s)


def kernel(x_prompt, x_sample, mem_prompt, cache_moba_k, cache_moba_v, page_table, state_gdn, state_gdn_conv, cache_mem_k, cache_mem_v, w_in, gdn_conv_w, gdn_a_log, gdn_dt_bias, gdn_norm_g, sgu_norm_g, sgu_norm_b, sgu_w, sgu_b, w_out_a, w_out_b, w_out_c, w_mix_out, w_xq, w_xk, w_xv, w_xo, w_router, b_router, w_gate_up, b_gate_up, w_down, b_down, ln_g, ln_b):
    bp, seq, _ = x_prompt.shape
    db, ts, _ = x_sample.shape
    n_pages = page_table.shape[1]
    n_pool, _, page, _, _ = cache_moba_k.shape
    mem_len = mem_prompt.shape[1]
    np_, ns = bp * seq, db * ts
    ckt = cache_moba_k.transpose(0, 1, 3, 4, 2).reshape(n_pool, DEPTH, A_WIDTH, page)
    cvt = cache_moba_v.transpose(0, 1, 3, 4, 2).reshape(n_pool, DEPTH, A_WIDTH, page)
    pt_flat = page_table.reshape(-1).astype(jnp.int32)
    mem2 = mem_prompt.reshape(bp * mem_len, D_MODEL)
    pad8 = lambda a: jnp.pad(a.reshape(db, ts, -1), ((0, 0), (0, 8 - ts), (0, 0)))
    gdn_npc = 2 if (seq // GDN_CHUNK) % 2 == 0 else 1

    yp = x_prompt.reshape(np_, D_MODEL)
    ys = x_sample.reshape(ns, D_MODEL)
    outs = {k: [] for k in ("kp", "vp", "ks", "vs", "sp", "ss", "cp", "cs", "mk", "mv", "vr")}
    for l in range(DEPTH):
        w_a, w_s, w_c, consts, sgu_p, sgu_s = _layer_consts(
            l, w_in, gdn_conv_w, gdn_a_log, gdn_dt_bias, sgu_norm_g, sgu_norm_b, sgu_w, sgu_b, ts)
        bd = consts[1]
        woa, wob, woc, wmo = _bf(w_out_a[l]), _bf(w_out_b[l]), _bf(w_out_c[l]), _bf(w_mix_out[l])
        wxq, wxo = _bf(w_xq[l]), _bf(w_xo[l])
        lg = [ln_g[l, i].reshape(1, -1) for i in range(3)]
        lb = [ln_b[l, i].reshape(1, -1) for i in range(3)]
        ng = jnp.tile(gdn_norm_g[l], B_HEADS).reshape(1, -1)
        wr_pad = jnp.pad(w_router[l], ((0, 0), (0, LANES - N_EXPERTS)))
        br_pad = jnp.pad(b_router[l], (0, LANES - N_EXPERTS), constant_values=NEG).reshape(1, LANES)

        mk = _matmul(mem2, _bf(w_xk[l]), 512, 512)
        mv = _matmul(mem2, _bf(w_xv[l]), 512, 512)
        hpa = _matmul(yp, w_a, 1024, W_A // 4)
        hps = _matmul(yp, w_s, 1024, LANES)
        hpc = _matmul(yp, w_c, 1024, W_C // 4)
        qn, kn, vv, bx, gx, yc = _post_prompt(hpa, hps, hpc, consts, sgu_p, bp, seq)
        hpa3 = hpa.reshape(bp, seq, W_A)
        ya = _moba_prompt(hpa3, bp, seq).reshape(np_, A_WIDTH)
        r3 = lambda a: a.reshape(bp, seq, 512)
        s0 = jnp.zeros((bp, HEAD_PAIRS, LANES, LANES), F32)
        yb, sp = _gdn(r3(qn), r3(kn), r3(vv), r3(bx), r3(gx), s0, GDN_CHUNK, bp, gdn_npc)
        x1 = _merge(ya, yb.reshape(np_, 512), hpa, yc, hpc, yp, bd, ng, woa, wob, woc, wmo, lg[0], lb[0])
        qx = _matmul(x1, wxq, 1024, D_MODEL)
        ox = _xattn(qx.reshape(bp, seq, D_MODEL), mk.reshape(bp, mem_len, D_MODEL),
                    mv.reshape(bp, mem_len, D_MODEL), 512).reshape(np_, D_MODEL)
        outs["kp"].append(hpa3[:, :, 512:1024].reshape(bp, seq, A_HEADS, A_HEAD_DIM))
        outs["vp"].append(hpa3[:, :, 1024:1536].reshape(bp, seq, A_HEADS, A_HEAD_DIM))
        outs["cp"].append(hpa3[:, seq - (CONV_W - 1):, 1536:3072])
        outs["sp"].append(_state_from_pairs(sp))
        outs["mk"].append(mk.reshape(bp, mem_len, X_HEADS, X_HEAD_DIM))
        outs["mv"].append(mv.reshape(bp, mem_len, X_HEADS, X_HEAD_DIM))

        hpa_s = _matmul(ys, w_a, 512, 512)
        hps_s = _matmul(ys, w_s, 512, LANES)
        hpc_s = _matmul(ys, w_c, 512, 512)
        qn_s, kn_s, v_s, bx_s, gx_s, yc_s, vr_s = _post_sample(
            hpa_s, hps_s, hpc_s, state_gdn_conv[:, l], consts, sgu_s, db, ts)
        hpa_s3 = hpa_s.reshape(db, ts, W_A)
        ya_s = _moba_sample(pad8(hpa_s3[:, :, :512]), pad8(hpa_s3[:, :, 512:1024]),
                            pad8(hpa_s3[:, :, 1024:1536]), ckt, cvt, pt_flat, l, db, n_pages, ts)
        ya_s = ya_s[:, :ts].reshape(ns, A_WIDTH)
        gdn_in = [pad8(a) for a in (qn_s, kn_s, v_s, bx_s, gx_s)]
        yb_s, ss = _gdn(*gdn_in, _state_to_pairs(state_gdn[:, l]), 8, 4, 1)
        yb_s = yb_s[:, :ts].reshape(ns, 512)
        x1s = _merge(ya_s, yb_s, hpa_s, yc_s.reshape(ns, 512), hpc_s, ys, bd, ng, woa, wob, woc, wmo,
                     lg[0], lb[0])
        qx_s = _matmul(x1s, wxq, 512, 512)
        ox_s = _xattn_cache(pad8(qx_s), cache_mem_k, cache_mem_v, l)
        ox_s = ox_s[:, :ts].reshape(ns, D_MODEL)
        x2 = _proj_res_ln_pair(ox, x1, ox_s, x1s, wxo, lg[1], lb[1], _tile(ns, 512))
        outs["ks"].append(hpa_s3[:, :, 512:1024].reshape(db, ts, A_HEADS, A_HEAD_DIM))
        outs["vs"].append(hpa_s3[:, :, 1024:1536].reshape(db, ts, A_HEADS, A_HEAD_DIM))
        outs["cs"].append(hpa_s3[:, ts - (CONV_W - 1):, 1536:3072])
        outs["ss"].append(_state_from_pairs(ss))
        outs["vr"].append(vr_s.reshape(db, ts, C_WIDTH))

        yp, ys = _moe(x2, np_, wr_pad, br_pad, w_gate_up, b_gate_up, w_down, b_down, l, lg[2], lb[2])

    st = lambda k: jnp.stack(outs[k], axis=1)
    return (yp.reshape(bp, seq, D_MODEL), ys.reshape(db, ts, D_MODEL), st("kp"), st("vp"), st("ks"),
            st("vs"), st("sp"), st("ss"), st("cp"), st("cs"), st("mk"), st("mv"), st("vr"))
```
